```python
import jax, jax.numpy as jnp
from jax import lax
import numpy as np

D_MODEL = 1024
BATCH = 32
SEQ = 2048
DEPTH = 1

CHUNK = 64
CONV_WIDTH = 3
CONV_DIM = D_MODEL // 2
CONV_GROUPS = 8
RET_HEADS = 4
RET_DIM = D_MODEL - CONV_DIM
RET_HEAD_DIM = RET_DIM // RET_HEADS
MIX_DIM = CONV_DIM + RET_DIM
IN_DIM = 3 * CONV_DIM + 4 * RET_DIM
D_FF = 4 * D_MODEL
ROPE_BASE = 10000.0
DECAY_OFFSET = 5.0
EPS = 1e-6
N_MOD = 6

kernel_name = 'hybrid_shortconv_retention_sandwich_adaln'


def rms_norm(x, g):
    xf = x.astype(jnp.float32)
    y = xf * lax.rsqrt(jnp.mean(xf * xf, axis=-1, keepdims=True) + EPS)
    return (y * g.astype(jnp.float32)).astype(x.dtype)


def rotary(t, positions):
    half = t.shape[-1] // 2
    inv_freq = ROPE_BASE ** (-jnp.arange(half, dtype=jnp.float32) / half)
    ang = positions.astype(jnp.float32)[..., None] * inv_freq
    cos = jnp.cos(ang)[:, :, None, :]
    sin = jnp.sin(ang)[:, :, None, :]
    tf = t.astype(jnp.float32)
    t1, t2 = tf[..., :half], tf[..., half:]
    out = jnp.concatenate([t1 * cos - t2 * sin, t2 * cos + t1 * sin], axis=-1)
    return out.astype(t.dtype)


def short_conv_mixer(xin, b_gate, c_gate, conv_w):
    S = xin.shape[1]
    u = c_gate * xin
    up = jnp.pad(u, ((0, 0), (CONV_WIDTH - 1, 0), (0, 0)))
    y = up[:, 0:S] * conv_w[0]
    for j in range(1, CONV_WIDTH):
        y = y + up[:, j:j + S] * conv_w[j]
    return b_gate * y


def retention_mixer(q, k, v, g, positions):
    B, S, _ = q.shape
    H, dh, C = RET_HEADS, RET_HEAD_DIM, CHUNK
    NC = S // C
    dt = q.dtype
    q = rotary(q.reshape(B, S, H, dh), positions)
    k = rotary(k.reshape(B, S, H, dh), positions) * (dh ** -0.5)
    v = v.reshape(B, S, H, dh)

    def to_chunks(t):
        return t.reshape(B, NC, C, H, dh).transpose(0, 3, 1, 2, 4)

    q, k, v = to_chunks(q), to_chunks(k), to_chunks(v)

    log_gamma = jnp.log1p(-jnp.exp2(-DECAY_OFFSET - jnp.arange(H, dtype=jnp.float32)))
    idx = jnp.arange(C, dtype=jnp.float32)
    intra_dec = jnp.exp(log_gamma[:, None, None] * jnp.abs(idx[:, None] - idx[None, :]))
    q_dec = jnp.exp(log_gamma[:, None] * (idx + 1.0))
    k_dec = jnp.exp(log_gamma[:, None] * (C - 1.0 - idx))
    chunk_dec = jnp.exp(log_gamma * C)

    scores = jnp.einsum('bhncd,bhnmd->bhncm', q, k) * intra_dec[:, None].astype(dt)
    o_intra = jnp.einsum('bhncm,bhnmd->bhncd', scores, v)

    kv = jnp.einsum('bhnmd,bhnme->nbhde', k * k_dec[:, None, :, None].astype(dt), v)
    kv = kv.astype(jnp.float32)

    def step(state, kv_n):
        return state * chunk_dec[:, None, None] + kv_n, state

    _, s_prev = lax.scan(step, jnp.zeros((B, H, dh, dh), jnp.float32), kv)
    o_cross = jnp.einsum('bhncd,nbhde->bhnce',
                         q * q_dec[:, None, :, None].astype(dt), s_prev.astype(dt))

    o = (o_intra + o_cross).transpose(0, 2, 3, 1, 4).reshape(B, S, H, dh)
    of = o.astype(jnp.float32)
    of = of * lax.rsqrt(jnp.mean(of * of, axis=-1, keepdims=True) + EPS)
    o = of.astype(dt).reshape(B, S, RET_DIM)
    return jax.nn.silu(g) * o


def setup_inputs(seed: int = 0) -> dict:
    key = jax.random.key(seed)
    ks = jax.random.split(key, 16)
    f32 = jnp.float32
    x = jax.random.normal(ks[0], (BATCH, SEQ, D_MODEL), f32)
    c = jax.random.normal(ks[1], (BATCH, D_MODEL), f32)
    offset = jax.random.randint(ks[2], (BATCH, 1), 0, 4096, dtype=jnp.int32)
    positions = offset + jnp.arange(SEQ, dtype=jnp.int32)[None, :]
    w_ada = jax.random.normal(ks[3], (DEPTH, D_MODEL, N_MOD * D_MODEL), f32) * (0.5 * D_MODEL ** -0.5)
    b_ada = jax.random.normal(ks[4], (DEPTH, N_MOD * D_MODEL), f32) * 0.01
    g_pre_mix = 1.0 + 0.05 * jax.random.normal(ks[5], (DEPTH, D_MODEL), f32)
    g_post_mix = 1.0 + 0.05 * jax.random.normal(ks[6], (DEPTH, D_MODEL), f32)
    w_in = jax.random.normal(ks[7], (DEPTH, D_MODEL, IN_DIM), f32) * D_MODEL ** -0.5
    conv_w = jax.random.normal(ks[8], (DEPTH, CONV_WIDTH, CONV_DIM), f32) * CONV_WIDTH ** -0.5
    w_out = jax.random.normal(ks[9], (DEPTH, MIX_DIM, D_MODEL), f32) * MIX_DIM ** -0.5
    g_pre_mlp = 1.0 + 0.05 * jax.random.normal(ks[10], (DEPTH, D_MODEL), f32)
    g_post_mlp = 1.0 + 0.05 * jax.random.normal(ks[11], (DEPTH, D_MODEL), f32)
    w_fc1 = jax.random.normal(ks[12], (DEPTH, D_MODEL, D_FF), f32) * D_MODEL ** -0.5
    w_fc2 = jax.random.normal(ks[13], (DEPTH, D_FF, D_MODEL), f32) * D_FF ** -0.5
    return {'x': x, 'c': c, 'positions': positions, 'w_ada': w_ada, 'b_ada': b_ada,
            'g_pre_mix': g_pre_mix, 'g_post_mix': g_post_mix, 'w_in': w_in,
            'conv_w': conv_w, 'w_out': w_out, 'g_pre_mlp': g_pre_mlp,
            'g_post_mlp': g_post_mlp, 'w_fc1': w_fc1, 'w_fc2': w_fc2}


def reference(x, c, positions, w_ada, b_ada, g_pre_mix, g_post_mix, w_in, conv_w,
              w_out, g_pre_mlp, g_post_mlp, w_fc1, w_fc2):
    split_at = [CONV_DIM, 2 * CONV_DIM, 3 * CONV_DIM,
                3 * CONV_DIM + RET_DIM, 3 * CONV_DIM + 2 * RET_DIM, 3 * CONV_DIM + 3 * RET_DIM]
    for layer in range(DEPTH):
        mod = jax.nn.silu(c) @ w_ada[layer] + b_ada[layer]
        shift1, scale1, gate1, shift2, scale2, gate2 = [
            m[:, None, :] for m in jnp.split(mod, N_MOD, axis=-1)]

        h = rms_norm(x, g_pre_mix[layer]) * (1.0 + scale1) + shift1
        proj = h @ w_in[layer]
        xin, b_gate, c_gate, q, k, v, g = jnp.split(proj, split_at, axis=-1)
        y_conv = short_conv_mixer(xin, b_gate, c_gate, conv_w[layer])
        y_ret = retention_mixer(q, k, v, g, positions)
        mix = jnp.concatenate([y_conv, y_ret], axis=-1) @ w_out[layer]
        x = x + gate1 * rms_norm(mix, g_post_mix[layer])

        h = rms_norm(x, g_pre_mlp[layer]) * (1.0 + scale2) + shift2
        f = jnp.square(jax.nn.relu(h @ w_fc1[layer])) @ w_fc2[layer]
        x = x + gate2 * rms_norm(f, g_post_mlp[layer])
    return x
```

```python
import functools
import math

import jax
import jax.numpy as jnp
from jax import lax
from jax.experimental import pallas as pl
from jax.experimental.pallas import tpu as pltpu

F32 = jnp.float32
BF16 = jnp.bfloat16

D_MODEL = 1024
CHUNK = 64
CONV_WIDTH = 3
CONV_DIM = D_MODEL // 2
RET_HEADS = 4
RET_DIM = D_MODEL - CONV_DIM
HEAD_DIM = RET_DIM // RET_HEADS
IN_DIM = 3 * CONV_DIM + 4 * RET_DIM
D_FF = 4 * D_MODEL
ROPE_BASE = 10000.0
DECAY_OFFSET = 5.0
EPS = 1e-6
N_MOD = 6

LANES = 128
SUBLANES = 8
SEQ_TILE = 512
RET_TILE = 256
MLP_TILE = 512
FF_BLOCK = 512
ADA_BLOCK = 768
VMEM_LIMIT = 52 * 1024 * 1024

_XIN, _BG, _CG = 0, CONV_DIM, 2 * CONV_DIM
_Q = 3 * CONV_DIM
_K = _Q + RET_DIM
_V = _K + RET_DIM
_G = _V + RET_DIM

_LOG_GAMMA = [math.log1p(-2.0 ** (-DECAY_OFFSET - h)) for h in range(RET_HEADS)]


def _dot(a, b):
    return jnp.dot(a, b, preferred_element_type=F32)


def _rms_scale(v):
    return lax.rsqrt(jnp.mean(v * v, axis=-1, keepdims=True) + EPS)


def _ada_kernel(c_ref, w_ref, b_ref, o_ref):
    c = c_ref[...]
    a = (c * jax.nn.sigmoid(c)).astype(BF16)
    o_ref[...] = _dot(a, w_ref[...].astype(BF16)) + b_ref[...]


def _ada(c, w_ada, b_ada):
    batch = c.shape[0]
    n = w_ada.shape[1]
    return pl.pallas_call(
        _ada_kernel,
        grid=(n // ADA_BLOCK,),
        in_specs=[
            pl.BlockSpec((batch, D_MODEL), lambda j: (0, 0)),
            pl.BlockSpec((D_MODEL, ADA_BLOCK), lambda j: (0, j)),
            pl.BlockSpec((1, ADA_BLOCK), lambda j: (0, j)),
        ],
        out_specs=pl.BlockSpec((batch, ADA_BLOCK), lambda j: (0, j)),
        out_shape=jax.ShapeDtypeStruct((batch, n), F32),
        compiler_params=pltpu.CompilerParams(dimension_semantics=("arbitrary",)),
        name="ada_mod",
    )(c, w_ada, b_ada.reshape(1, n))


def _fill_decay_tables(dmat_ref, qdec_ref, kdec_ref):
    i = lax.broadcasted_iota(jnp.int32, (RET_TILE, RET_TILE), 0)
    j = lax.broadcasted_iota(jnp.int32, (RET_TILE, RET_TILE), 1)
    ci = i // CHUNK
    cj = j // CHUNK
    dist = jnp.where(ci == cj, jnp.abs(i - j), i - j).astype(F32)
    r = lax.broadcasted_iota(jnp.int32, (RET_TILE, HEAD_DIM), 0).astype(F32)
    for h in range(RET_HEADS):
        lg = _LOG_GAMMA[h]
        dmat_ref[h] = jnp.where(cj <= ci, jnp.exp(lg * dist), 0.0)
        qdec_ref[h] = jnp.exp(lg * (r + 1.0))
        kdec_ref[h] = jnp.exp(lg * (RET_TILE - 1.0 - r))


def _mixer_kernel(x_ref, mod_ref, pos_ref, gpre_ref, gpost_ref, win_ref, convw_ref, wout_ref,
                  o_ref, proj_ref, ymix_ref, state_ref, ucarry_ref, dmat_ref, qdec_ref, kdec_ref):
    ti = pl.program_id(1)

    @pl.when(jnp.logical_and(pl.program_id(0) == 0, ti == 0))
    def _():
        _fill_decay_tables(dmat_ref, qdec_ref, kdec_ref)

    @pl.when(ti == 0)
    def _():
        state_ref[...] = jnp.zeros_like(state_ref)
        ucarry_ref[...] = jnp.zeros_like(ucarry_ref)

    shift = mod_ref[0, 0:1, :]
    scale = mod_ref[0, 1:2, :]
    gate = mod_ref[0, 2:3, :]

    x = x_ref[0]
    h = (x * _rms_scale(x)) * (gpre_ref[...] * (1.0 + scale)) + shift
    proj_ref[...] = _dot(h.astype(BF16), win_ref[...])

    u = proj_ref[:, _CG:_CG + CONV_DIM] * proj_ref[:, _XIN:_XIN + CONV_DIM]
    carry = ucarry_ref[...]
    row = lax.broadcasted_iota(jnp.int32, (SUBLANES, CONV_DIM), 0)
    u1 = pltpu.roll(u, 1, axis=0)
    u2 = pltpu.roll(u, 2, axis=0)
    c1 = pltpu.roll(carry, 1, axis=0)
    c2 = pltpu.roll(carry, 2, axis=0)
    u1 = jnp.concatenate([jnp.where(row < 1, c1, u1[:SUBLANES]), u1[SUBLANES:]], axis=0)
    u2 = jnp.concatenate([jnp.where(row < 2, c2, u2[:SUBLANES]), u2[SUBLANES:]], axis=0)
    y_conv = proj_ref[:, _BG:_BG + CONV_DIM] * (
        convw_ref[0:1, :] * u2 + convw_ref[1:2, :] * u1 + convw_ref[2:3, :] * u)
    ymix_ref[:, 0:CONV_DIM] = y_conv.astype(BF16)
    ucarry_ref[...] = u[SEQ_TILE - SUBLANES:, :]

    half = HEAD_DIM // 2
    lane = lax.broadcasted_iota(jnp.int32, (1, HEAD_DIM), 1)
    inv_freq = jnp.exp((lane % half).astype(F32) * (-math.log(ROPE_BASE) / half))
    sign = jnp.where(lane < half, -1.0, 1.0)

    for s in range(SEQ_TILE // RET_TILE):
        r0 = s * RET_TILE
        pos_rows = []
        for blk in range(RET_TILE // LANES):
            prow = pos_ref[0, pl.ds(ti * (SEQ_TILE // LANES) + s * (RET_TILE // LANES) + blk, 1), :]
            pos_rows.append(jnp.broadcast_to(prow, (LANES, LANES)).T)
        pos_col = jnp.concatenate(pos_rows, axis=0)
        ang = pos_col * inv_freq
        cos2 = jnp.cos(ang)
        sin2 = jnp.sin(ang) * sign

        for hd in range(RET_HEADS):
            c0 = hd * HEAD_DIM
            q = proj_ref[r0:r0 + RET_TILE, _Q + c0:_Q + c0 + HEAD_DIM]
            k = proj_ref[r0:r0 + RET_TILE, _K + c0:_K + c0 + HEAD_DIM]
            v = proj_ref[r0:r0 + RET_TILE, _V + c0:_V + c0 + HEAD_DIM].astype(BF16)
            g = proj_ref[r0:r0 + RET_TILE, _G + c0:_G + c0 + HEAD_DIM]
            qr = q * cos2 + pltpu.roll(q, half, axis=1) * sin2
            kr = (k * cos2 + pltpu.roll(k, half, axis=1) * sin2) * (HEAD_DIM ** -0.5)

            scores = lax.dot_general(qr.astype(BF16), kr.astype(BF16), (((1,), (1,)), ((), ())),
                                     preferred_element_type=F32)
            p = (scores * dmat_ref[hd]).astype(BF16)
            state = state_ref[hd]
            o = _dot(p, v) + _dot((qr * qdec_ref[hd]).astype(BF16), state.astype(BF16))
            kd_t = (kr * kdec_ref[hd]).T.astype(BF16)
            state_ref[hd] = state * math.exp(_LOG_GAMMA[hd] * RET_TILE) + _dot(kd_t, v)

            o = o * _rms_scale(o)
            y = (g * jax.nn.sigmoid(g)) * o
            ymix_ref[r0:r0 + RET_TILE, CONV_DIM + c0:CONV_DIM + c0 + HEAD_DIM] = y.astype(BF16)

    mix = _dot(ymix_ref[...], wout_ref[...])
    x = x_ref[0]
    o_ref[0] = x + (mix * _rms_scale(mix)) * (gate * gpost_ref[...])


def _mixer(x, mod, pos, g_pre, g_post, w_in, conv_w, w_out):
    batch, seq, _ = x.shape
    const = dict(pipeline_mode=pl.Buffered(1))
    return pl.pallas_call(
        _mixer_kernel,
        grid=(batch, seq // SEQ_TILE),
        in_specs=[
            pl.BlockSpec((1, SEQ_TILE, D_MODEL), lambda b, t: (b, t, 0)),
            pl.BlockSpec((1, N_MOD, D_MODEL), lambda b, t: (b, 0, 0)),
            pl.BlockSpec((1, seq // LANES, LANES), lambda b, t: (b, 0, 0)),
            pl.BlockSpec((1, D_MODEL), lambda b, t: (0, 0)),
            pl.BlockSpec((1, D_MODEL), lambda b, t: (0, 0)),
            pl.BlockSpec((D_MODEL, IN_DIM), lambda b, t: (0, 0), **const),
            pl.BlockSpec((CONV_WIDTH, CONV_DIM), lambda b, t: (0, 0)),
            pl.BlockSpec((D_MODEL, D_MODEL), lambda b, t: (0, 0), **const),
        ],
        out_specs=pl.BlockSpec((1, SEQ_TILE, D_MODEL), lambda b, t: (b, t, 0)),
        out_shape=jax.ShapeDtypeStruct(x.shape, F32),
        scratch_shapes=[
            pltpu.VMEM((SEQ_TILE, IN_DIM), F32),
            pltpu.VMEM((SEQ_TILE, D_MODEL), BF16),
            pltpu.VMEM((RET_HEADS, HEAD_DIM, HEAD_DIM), F32),
            pltpu.VMEM((SUBLANES, CONV_DIM), F32),
            pltpu.VMEM((RET_HEADS, RET_TILE, RET_TILE), F32),
            pltpu.VMEM((RET_HEADS, RET_TILE, HEAD_DIM), F32),
            pltpu.VMEM((RET_HEADS, RET_TILE, HEAD_DIM), F32),
        ],
        compiler_params=pltpu.CompilerParams(
            dimension_semantics=("arbitrary", "arbitrary"), vmem_limit_bytes=VMEM_LIMIT),
        name="mixer",
    )(x, mod, pos, g_pre, g_post, w_in, conv_w, w_out)


def _mlp_kernel(x_ref, mod_ref, gpre_ref, gpost_ref, w1_ref, w2_ref, o_ref, act_ref):
    shift = mod_ref[0, 3:4, :]
    scale = mod_ref[0, 4:5, :]
    gate = mod_ref[0, 5:6, :]
    x = x_ref[0]
    h = ((x * _rms_scale(x)) * (gpre_ref[...] * (1.0 + scale)) + shift).astype(BF16)
    for j in range(D_FF // FF_BLOCK):
        a = jnp.maximum(_dot(h, w1_ref[:, j * FF_BLOCK:(j + 1) * FF_BLOCK]), 0.0)
        act_ref[:, j * FF_BLOCK:(j + 1) * FF_BLOCK] = (a * a).astype(BF16)
    f = _dot(act_ref[...], w2_ref[...])
    o_ref[0] = x_ref[0] + (f * _rms_scale(f)) * (gate * gpost_ref[...])


def _mlp(x, mod, g_pre, g_post, w1, w2):
    batch, seq, _ = x.shape
    const = dict(pipeline_mode=pl.Buffered(1))
    return pl.pallas_call(
        _mlp_kernel,
        grid=(batch, seq // MLP_TILE),
        in_specs=[
            pl.BlockSpec((1, MLP_TILE, D_MODEL), lambda b, t: (b, t, 0)),
            pl.BlockSpec((1, N_MOD, D_MODEL), lambda b, t: (b, 0, 0)),
            pl.BlockSpec((1, D_MODEL), lambda b, t: (0, 0)),
            pl.BlockSpec((1, D_MODEL), lambda b, t: (0, 0)),
            pl.BlockSpec((D_MODEL, D_FF), lambda b, t: (0, 0), **const),
            pl.BlockSpec((D_FF, D_MODEL), lambda b, t: (0, 0), **const),
        ],
        out_specs=pl.BlockSpec((1, MLP_TILE, D_MODEL), lambda b, t: (b, t, 0)),
        out_shape=jax.ShapeDtypeStruct(x.shape, F32),
        scratch_shapes=[pltpu.VMEM((MLP_TILE, D_FF), BF16)],
        compiler_params=pltpu.CompilerParams(
            dimension_semantics=("arbitrary", "arbitrary"), vmem_limit_bytes=VMEM_LIMIT),
        name="mlp",
    )(x, mod, g_pre, g_post, w1, w2)


def kernel(x, c, positions, w_ada, b_ada, g_pre_mix, g_post_mix, w_in, conv_w, w_out,
           g_pre_mlp, g_post_mlp, w_fc1, w_fc2):
    batch, seq, d = x.shape
    assert d == D_MODEL and seq % SEQ_TILE == 0 and seq % MLP_TILE == 0
    assert SEQ_TILE % RET_TILE == 0 and RET_TILE % CHUNK == 0 and RET_TILE % LANES == 0
    pos = positions.astype(F32).reshape(batch, seq // LANES, LANES)
    for layer in range(w_ada.shape[0]):
        mod = _ada(c, w_ada[layer], b_ada[layer]).reshape(batch, N_MOD, D_MODEL)
        x = _mixer(x, mod, pos, g_pre_mix[layer][None], g_post_mix[layer][None],
                   w_in[layer].astype(BF16), conv_w[layer], w_out[layer].astype(BF16))
        x = _mlp(x, mod, g_pre_mlp[layer][None], g_post_mlp[layer][None],
                 w_fc1[layer].astype(BF16), w_fc2[layer].astype(BF16))
    return x
```

```python
import math
import types

import jax
import jax.numpy as jnp
from jax import lax
from jax.experimental import pallas as pl
from jax.experimental.pallas import tpu as pltpu

F32 = jnp.float32
BF16 = jnp.bfloat16

D_MODEL = 1024
CHUNK = 64
CONV_WIDTH = 3
CONV_DIM = D_MODEL // 2
RET_HEADS = 4
RET_DIM = D_MODEL - CONV_DIM
HEAD_DIM = RET_DIM // RET_HEADS
IN_DIM = 3 * CONV_DIM + 4 * RET_DIM
D_FF = 4 * D_MODEL
ROPE_BASE = 10000.0
DECAY_OFFSET = 5.0
EPS = 1e-6
N_MOD = 6

LANES = 128
SUBLANES = 8
SEQ_TILE = 512
RET_TILE = 256
ROW_BLOCK = 32
MLP_TILE = 512
FF_BLOCK = 512
ADA_BLOCK = 768
VMEM_LIMIT = 52 * 1024 * 1024

_XIN, _BG, _CG = 0, CONV_DIM, 2 * CONV_DIM
_Q = 3 * CONV_DIM
_K = _Q + RET_DIM
_V = _K + RET_DIM
_G = _V + RET_DIM

_LOG_GAMMA = [math.log1p(-2.0 ** (-DECAY_OFFSET - h)) for h in range(RET_HEADS)]


def _dot(a, b):
    return jnp.dot(a, b, preferred_element_type=F32)


def _rms_scale(v):
    return lax.rsqrt(jnp.mean(v * v, axis=-1, keepdims=True) + EPS)


def _ada_kernel(c_ref, w_ref, b_ref, o_ref):
    c = c_ref[...]
    a = (c * jax.nn.sigmoid(c)).astype(BF16)
    o_ref[...] = _dot(a, w_ref[...].astype(BF16)) + b_ref[...]


def _ada(c, w_ada, b_ada):
    batch = c.shape[0]
    n = w_ada.shape[1]
    return pl.pallas_call(
        _ada_kernel,
        grid=(n // ADA_BLOCK,),
        in_specs=[
            pl.BlockSpec((batch, D_MODEL), lambda j: (0, 0)),
            pl.BlockSpec((D_MODEL, ADA_BLOCK), lambda j: (0, j)),
            pl.BlockSpec((1, ADA_BLOCK), lambda j: (0, j)),
        ],
        out_specs=pl.BlockSpec((batch, ADA_BLOCK), lambda j: (0, j)),
        out_shape=jax.ShapeDtypeStruct((batch, n), F32),
        compiler_params=pltpu.CompilerParams(dimension_semantics=("arbitrary",)),
        name="ada_mod",
    )(c, w_ada, b_ada.reshape(1, n))


def _inv_freq():
    half = HEAD_DIM // 2
    lane = lax.broadcasted_iota(jnp.int32, (1, HEAD_DIM), 1)
    return jnp.exp((lane % half).astype(F32) * (-math.log(ROPE_BASE) / half))


def _fill_tables(r):
    k_scale = HEAD_DIM ** -0.5
    i = lax.broadcasted_iota(jnp.int32, (RET_TILE, RET_TILE), 0)
    j = lax.broadcasted_iota(jnp.int32, (RET_TILE, RET_TILE), 1)
    ci = i // CHUNK
    cj = j // CHUNK
    dist = jnp.where(ci == cj, jnp.abs(i - j), i - j).astype(F32)
    t = lax.broadcasted_iota(jnp.int32, (RET_TILE, HEAD_DIM), 0).astype(F32)
    for h in range(RET_HEADS):
        lg = _LOG_GAMMA[h]
        r.dmat[h] = jnp.where(cj <= ci, k_scale * jnp.exp(lg * dist), 0.0)
        r.qdec[h] = jnp.exp(lg * (t + 1.0))
        r.kdec[h] = k_scale * jnp.exp(lg * (RET_TILE - 1.0 - t))
    ang = t * _inv_freq()
    r.cosr[...] = jnp.cos(ang)
    r.sinr[...] = jnp.sin(ang)


def _norm(r):
    gain = r.gpre[...] * (1.0 + r.mod[0, 1:2, :])
    shift = r.mod[0, 0:1, :]
    for r0 in range(0, SEQ_TILE, ROW_BLOCK):
        rows = slice(r0, r0 + ROW_BLOCK)
        x = r.x[0, rows, :]
        r.h[rows, 0:D_MODEL] = ((x * _rms_scale(x)) * gain + shift).astype(BF16)


def _finish(r):
    gain = r.mod[0, 2:3, :] * r.gpost[...]
    for r0 in range(0, SEQ_TILE, ROW_BLOCK):
        rows = slice(r0, r0 + ROW_BLOCK)
        mix = r.mix[rows, 0:D_MODEL]
        r.out[0, rows, :] = r.x[0, rows, :] + (mix * _rms_scale(mix)) * gain


def _conv(r, fresh):
    tail = jnp.where(fresh, 0.0, r.ucarry[...])
    w0, w1, w2 = r.convw[0:1, :], r.convw[1:2, :], r.convw[2:3, :]
    for r0 in range(0, SEQ_TILE, ROW_BLOCK):
        rows = slice(r0, r0 + ROW_BLOCK)
        u = r.proj[rows, _CG:_CG + CONV_DIM] * r.proj[rows, _XIN:_XIN + CONV_DIM]
        ext = jnp.concatenate([tail, u], axis=0)
        u1 = pltpu.roll(ext, 1, axis=0)[SUBLANES:]
        u2 = pltpu.roll(ext, 2, axis=0)[SUBLANES:]
        y_conv = r.proj[rows, _BG:_BG + CONV_DIM] * (w0 * u2 + w1 * u1 + w2 * u)
        r.ymix[rows, 0:CONV_DIM] = y_conv.astype(BF16)
        tail = u[ROW_BLOCK - SUBLANES:]
    r.ucarry[...] = tail


def _retention(r, tile_in_seq, fresh):
    half = HEAD_DIM // 2
    lane = lax.broadcasted_iota(jnp.int32, (1, HEAD_DIM), 1)
    sign = jnp.where(lane < half, -1.0, 1.0)
    inv_freq = _inv_freq()

    for s in range(SEQ_TILE // RET_TILE):
        rows = slice(s * RET_TILE, (s + 1) * RET_TILE)
        p0 = r.pos[0, pl.ds(tile_in_seq * (SEQ_TILE // LANES) + s * (RET_TILE // LANES), 1), 0:1]
        base = p0 * inv_freq
        cos_b = jnp.cos(base)
        sin_b = jnp.sin(base)
        for r0 in range(0, RET_TILE, ROW_BLOCK):
            blk = slice(r0, r0 + ROW_BLOCK)
            cosr = r.cosr[blk, :]
            sinr = r.sinr[blk, :]
            r.cos2[blk, :] = cosr * cos_b - sinr * sin_b
            r.sin2[blk, :] = sinr * (cos_b * sign) + cosr * (sin_b * sign)

        for hd in range(RET_HEADS):
            c0 = hd * HEAD_DIM
            q = r.proj[rows, _Q + c0:_Q + c0 + HEAD_DIM]
            k = r.proj[rows, _K + c0:_K + c0 + HEAD_DIM]
            v = r.proj[rows, _V + c0:_V + c0 + HEAD_DIM].astype(BF16)
            g = r.proj[rows, _G + c0:_G + c0 + HEAD_DIM]
            qr = q * r.cos2[...] + pltpu.roll(q, half, axis=1) * r.sin2[...]
            kr = k * r.cos2[...] + pltpu.roll(k, half, axis=1) * r.sin2[...]
            scores = lax.dot_general(qr.astype(BF16), kr.astype(BF16), (((1,), (1,)), ((), ())),
                                     preferred_element_type=F32)
            p = (scores * r.dmat[hd]).astype(BF16)
            state = r.state[hd]
            if s == 0:
                state = jnp.where(fresh, 0.0, state)
            o = _dot(p, v) + _dot((qr * r.qdec[hd]).astype(BF16), state.astype(BF16))
            kd_t = (kr * r.kdec[hd]).T.astype(BF16)
            r.state[hd] = state * math.exp(_LOG_GAMMA[hd] * RET_TILE) + _dot(kd_t, v)
            y = (g * jax.nn.sigmoid(g)) * (o * _rms_scale(o))
            r.ymix[rows, CONV_DIM + c0:CONV_DIM + c0 + HEAD_DIM] = y.astype(BF16)


def _mixer_kernel(x, mod, pos, gpre, gpost, win, convw, wout, out,
                  h, proj, ymix, mix, state, ucarry, dmat, qdec, kdec, cosr, sinr, cos2, sin2):
    tile_in_seq = pl.program_id(1)
    r = types.SimpleNamespace(
        x=x, mod=mod, pos=pos, gpre=gpre, gpost=gpost, convw=convw, out=out, h=h, proj=proj, ymix=ymix,
        mix=mix, state=state, ucarry=ucarry, dmat=dmat, qdec=qdec, kdec=kdec, cosr=cosr, sinr=sinr,
        cos2=cos2, sin2=sin2)

    @pl.when(jnp.logical_and(pl.program_id(0) == 0, tile_in_seq == 0))
    def _():
        _fill_tables(r)
        state[...] = jnp.zeros_like(state)
        ucarry[...] = jnp.zeros_like(ucarry)

    fresh = tile_in_seq == 0
    _norm(r)
    proj[:, 0:IN_DIM] = _dot(h[:, 0:D_MODEL], win[...])
    _conv(r, fresh)
    _retention(r, tile_in_seq, fresh)
    mix[:, 0:D_MODEL] = _dot(ymix[:, 0:D_MODEL], wout[...])
    _finish(r)


def _mixer(x, mod, pos, g_pre, g_post, w_in, conv_w, w_out):
    batch, seq, _ = x.shape
    const = dict(pipeline_mode=pl.Buffered(1))
    fixed = lambda b, t: (0, 0)
    tile = (SEQ_TILE, D_MODEL)
    padded = (SEQ_TILE, D_MODEL + LANES)
    return pl.pallas_call(
        _mixer_kernel,
        grid=(batch, seq // SEQ_TILE),
        in_specs=[
            pl.BlockSpec((1,) + tile, lambda b, t: (b, t, 0)),
            pl.BlockSpec((1, N_MOD, D_MODEL), lambda b, t: (b, 0, 0)),
            pl.BlockSpec((1, seq // LANES, LANES), lambda b, t: (b, 0, 0)),
            pl.BlockSpec((1, D_MODEL), fixed),
            pl.BlockSpec((1, D_MODEL), fixed),
            pl.BlockSpec((D_MODEL, IN_DIM), fixed, **const),
            pl.BlockSpec((CONV_WIDTH, CONV_DIM), fixed),
            pl.BlockSpec((D_MODEL, D_MODEL), fixed, **const),
        ],
        out_specs=pl.BlockSpec((1,) + tile, lambda b, t: (b, t, 0)),
        out_shape=jax.ShapeDtypeStruct(x.shape, F32),
        scratch_shapes=[
            pltpu.VMEM(padded, BF16),
            pltpu.VMEM((SEQ_TILE, IN_DIM + LANES), F32),
            pltpu.VMEM(padded, BF16),
            pltpu.VMEM(padded, F32),
            pltpu.VMEM((RET_HEADS, HEAD_DIM, HEAD_DIM), F32),
            pltpu.VMEM((SUBLANES, CONV_DIM), F32),
            pltpu.VMEM((RET_HEADS, RET_TILE, RET_TILE), F32),
            pltpu.VMEM((RET_HEADS, RET_TILE, HEAD_DIM), F32),
            pltpu.VMEM((RET_HEADS, RET_TILE, HEAD_DIM), F32),
            pltpu.VMEM((RET_TILE, HEAD_DIM), F32),
            pltpu.VMEM((RET_TILE, HEAD_DIM), F32),
            pltpu.VMEM((RET_TILE, HEAD_DIM), F32),
            pltpu.VMEM((RET_TILE, HEAD_DIM), F32),
        ],
        compiler_params=pltpu.CompilerParams(
            dimension_semantics=("arbitrary", "arbitrary"), vmem_limit_bytes=VMEM_LIMIT),
        name="mixer",
    )(x, mod, pos, g_pre, g_post, w_in, conv_w, w_out)


def _mlp_kernel(x_ref, mod_ref, gpre_ref, gpost_ref, w1_ref, w2_ref, o_ref, act_ref):
    shift = mod_ref[0, 3:4, :]
    scale = mod_ref[0, 4:5, :]
    gate = mod_ref[0, 5:6, :]
    x = x_ref[0]
    h = ((x * _rms_scale(x)) * (gpre_ref[...] * (1.0 + scale)) + shift).astype(BF16)
    for j in range(D_FF // FF_BLOCK):
        a = jnp.maximum(_dot(h, w1_ref[:, j * FF_BLOCK:(j + 1) * FF_BLOCK]), 0.0)
        act_ref[:, j * FF_BLOCK:(j + 1) * FF_BLOCK] = (a * a).astype(BF16)
    f = _dot(act_ref[:, 0:D_FF], w2_ref[...])
    o_ref[0] = x_ref[0] + (f * _rms_scale(f)) * (gate * gpost_ref[...])


def _mlp(x, mod, g_pre, g_post, w1, w2):
    batch, seq, _ = x.shape
    const = dict(pipeline_mode=pl.Buffered(1))
    return pl.pallas_call(
        _mlp_kernel,
        grid=(batch, seq // MLP_TILE),
        in_specs=[
            pl.BlockSpec((1, MLP_TILE, D_MODEL), lambda b, t: (b, t, 0)),
            pl.BlockSpec((1, N_MOD, D_MODEL), lambda b, t: (b, 0, 0)),
            pl.BlockSpec((1, D_MODEL), lambda b, t: (0, 0)),
            pl.BlockSpec((1, D_MODEL), lambda b, t: (0, 0)),
            pl.BlockSpec((D_MODEL, D_FF), lambda b, t: (0, 0), **const),
            pl.BlockSpec((D_FF, D_MODEL), lambda b, t: (0, 0), **const),
        ],
        out_specs=pl.BlockSpec((1, MLP_TILE, D_MODEL), lambda b, t: (b, t, 0)),
        out_shape=jax.ShapeDtypeStruct(x.shape, F32),
        scratch_shapes=[pltpu.VMEM((MLP_TILE, D_FF + LANES), BF16)],
        compiler_params=pltpu.CompilerParams(
            dimension_semantics=("arbitrary", "arbitrary"), vmem_limit_bytes=VMEM_LIMIT),
        name="mlp",
    )(x, mod, g_pre, g_post, w1, w2)


def kernel(x, c, positions, w_ada, b_ada, g_pre_mix, g_post_mix, w_in, conv_w, w_out,
           g_pre_mlp, g_post_mlp, w_fc1, w_fc2):
    batch, seq, d = x.shape
    assert d == D_MODEL and seq % SEQ_TILE == 0 and seq % MLP_TILE == 0
    assert SEQ_TILE % RET_TILE == 0 and RET_TILE % CHUNK == 0 and RET_TILE % LANES == 0
    pos = positions.astype(F32).reshape(batch, seq // LANES, LANES)
    for layer in range(w_ada.shape[0]):
        mod = _ada(c, w_ada[layer], b_ada[layer]).reshape(batch, N_MOD, D_MODEL)
        x = _mixer(x, mod, pos, g_pre_mix[layer][None], g_post_mix[layer][None],
                   w_in[layer].astype(BF16), conv_w[layer], w_out[layer].astype(BF16))
        x = _mlp(x, mod, g_pre_mlp[layer][None], g_post_mlp[layer][None],
                 w_fc1[layer].astype(BF16), w_fc2[layer].astype(BF16))
    return x
```

```python
import math
import types

import jax
import jax.numpy as jnp
from jax import lax
from jax.experimental import pallas as pl
from jax.experimental.pallas import tpu as pltpu

F32 = jnp.float32
BF16 = jnp.bfloat16

D_MODEL = 1024
CHUNK = 64
CONV_WIDTH = 3
CONV_DIM = D_MODEL // 2
RET_HEADS = 4
RET_DIM = D_MODEL - CONV_DIM
HEAD_DIM = RET_DIM // RET_HEADS
IN_DIM = 3 * CONV_DIM + 4 * RET_DIM
D_FF = 4 * D_MODEL
ROPE_BASE = 10000.0
DECAY_OFFSET = 5.0
EPS = 1e-6
N_MOD = 6

LANES = 128
SUBLANES = 8
SEQ_TILE = 512
RET_TILE = 256
ROW_BLOCK = 32
MLP_TILE = 512
FF_BLOCK = 512
ADA_BLOCK = 768
VMEM_LIMIT = 52 * 1024 * 1024

_XIN, _BG, _CG = 0, CONV_DIM, 2 * CONV_DIM
_Q = 3 * CONV_DIM
_K = _Q + RET_DIM
_V = _K + RET_DIM
_G = _V + RET_DIM

_LOG_GAMMA = [math.log1p(-2.0 ** (-DECAY_OFFSET - h)) for h in range(RET_HEADS)]


def _dot(a, b):
    return jnp.dot(a, b, preferred_element_type=F32)


def _rms_scale(v):
    return lax.rsqrt(jnp.mean(v * v, axis=-1, keepdims=True) + EPS)


def _ada_kernel(c_ref, w_ref, b_ref, o_ref):
    c = c_ref[...]
    a = (c * jax.nn.sigmoid(c)).astype(BF16)
    o_ref[...] = _dot(a, w_ref[...].astype(BF16)) + b_ref[...]


def _ada(c, w_ada, b_ada):
    batch = c.shape[0]
    n = w_ada.shape[1]
    return pl.pallas_call(
        _ada_kernel,
        grid=(n // ADA_BLOCK,),
        in_specs=[
            pl.BlockSpec((batch, D_MODEL), lambda j: (0, 0)),
            pl.BlockSpec((D_MODEL, ADA_BLOCK), lambda j: (0, j)),
            pl.BlockSpec((1, ADA_BLOCK), lambda j: (0, j)),
        ],
        out_specs=pl.BlockSpec((batch, ADA_BLOCK), lambda j: (0, j)),
        out_shape=jax.ShapeDtypeStruct((batch, n), F32),
        compiler_params=pltpu.CompilerParams(dimension_semantics=("arbitrary",)),
        name="ada_mod",
    )(c, w_ada, b_ada.reshape(1, n))


def _inv_freq():
    half = HEAD_DIM // 2
    lane = lax.broadcasted_iota(jnp.int32, (1, HEAD_DIM), 1)
    return jnp.exp((lane % half).astype(F32) * (-math.log(ROPE_BASE) / half))


def _fill_tables(r):
    k_scale = HEAD_DIM ** -0.5
    i = lax.broadcasted_iota(jnp.int32, (RET_TILE, RET_TILE), 0)
    j = lax.broadcasted_iota(jnp.int32, (RET_TILE, RET_TILE), 1)
    ci = i // CHUNK
    cj = j // CHUNK
    dist = jnp.where(ci == cj, jnp.abs(i - j), i - j).astype(F32)
    t = lax.broadcasted_iota(jnp.int32, (RET_TILE, HEAD_DIM), 0).astype(F32)
    for h in range(RET_HEADS):
        lg = _LOG_GAMMA[h]
        r.dmat[h] = jnp.where(cj <= ci, k_scale * jnp.exp(lg * dist), 0.0)
        r.qdec[h] = jnp.exp(lg * (t + 1.0))
        r.kdec[h] = k_scale * jnp.exp(lg * (RET_TILE - 1.0 - t))
    ang = t * _inv_freq()
    r.cosr[...] = jnp.cos(ang)
    r.sinr[...] = jnp.sin(ang)


def _norm(r):
    gain = r.gpre[...] * (1.0 + r.mod[0, 1:2, :])
    shift = r.mod[0, 0:1, :]
    for r0 in range(0, SEQ_TILE, ROW_BLOCK):
        rows = slice(r0, r0 + ROW_BLOCK)
        x = r.x[0, rows, :]
        r.h[rows, 0:D_MODEL] = ((x * _rms_scale(x)) * gain + shift).astype(BF16)


def _finish(r):
    gain = r.mod[0, 2:3, :] * r.gpost[...]
    for r0 in range(0, SEQ_TILE, ROW_BLOCK):
        rows = slice(r0, r0 + ROW_BLOCK)
        mix = r.mix[rows, 0:D_MODEL]
        r.out[0, rows, :] = r.x[0, rows, :] + (mix * _rms_scale(mix)) * gain


def _conv(r, fresh):
    tail = jnp.where(fresh, 0.0, r.ucarry[...])
    w0, w1, w2 = r.convw[0:1, :], r.convw[1:2, :], r.convw[2:3, :]
    for r0 in range(0, SEQ_TILE, ROW_BLOCK):
        rows = slice(r0, r0 + ROW_BLOCK)
        u = r.proj_conv[rows, _CG:_CG + CONV_DIM] * r.proj_conv[rows, _XIN:_XIN + CONV_DIM]
        ext = jnp.concatenate([tail, u], axis=0)
        u1 = pltpu.roll(ext, 1, axis=0)[SUBLANES:]
        u2 = pltpu.roll(ext, 2, axis=0)[SUBLANES:]
        y_conv = r.proj_conv[rows, _BG:_BG + CONV_DIM] * (w0 * u2 + w1 * u1 + w2 * u)
        r.ymix[rows, 0:CONV_DIM] = y_conv.astype(BF16)
        tail = u[ROW_BLOCK - SUBLANES:]
    r.ucarry[...] = tail


def _retention(r, tile_in_seq, fresh):
    half = HEAD_DIM // 2
    lane = lax.broadcasted_iota(jnp.int32, (1, HEAD_DIM), 1)
    sign = jnp.where(lane < half, -1.0, 1.0)
    inv_freq = _inv_freq()

    for s in range(SEQ_TILE // RET_TILE):
        rows = slice(s * RET_TILE, (s + 1) * RET_TILE)
        p0 = r.pos[0, pl.ds(tile_in_seq * (SEQ_TILE // LANES) + s * (RET_TILE // LANES), 1), 0:1]
        base = p0 * inv_freq
        cos_b = jnp.cos(base)
        sin_b = jnp.sin(base)
        for r0 in range(0, RET_TILE, ROW_BLOCK):
            blk = slice(r0, r0 + ROW_BLOCK)
            cosr = r.cosr[blk, :]
            sinr = r.sinr[blk, :]
            r.cos2[blk, :] = cosr * cos_b - sinr * sin_b
            r.sin2[blk, :] = sinr * (cos_b * sign) + cosr * (sin_b * sign)

        for hd in range(RET_HEADS):
            c0 = hd * HEAD_DIM
            q = r.proj[rows, _Q + c0:_Q + c0 + HEAD_DIM]
            k = r.proj[rows, _K + c0:_K + c0 + HEAD_DIM]
            v = r.proj[rows, _V + c0:_V + c0 + HEAD_DIM].astype(BF16)
            g = r.proj[rows, _G + c0:_G + c0 + HEAD_DIM]
            qr = q * r.cos2[...] + pltpu.roll(q, half, axis=1) * r.sin2[...]
            kr = k * r.cos2[...] + pltpu.roll(k, half, axis=1) * r.sin2[...]
            scores = lax.dot_general(qr.astype(BF16), kr.astype(BF16), (((1,), (1,)), ((), ())),
                                     preferred_element_type=F32)
            p = (scores * r.dmat[hd]).astype(BF16)
            state = r.state[hd]
            if s == 0:
                state = jnp.where(fresh, 0.0, state)
            o = _dot(p, v) + _dot((qr * r.qdec[hd]).astype(BF16), state.astype(BF16))
            kd_t = (kr * r.kdec[hd]).T.astype(BF16)
            r.state[hd] = state * math.exp(_LOG_GAMMA[hd] * RET_TILE) + _dot(kd_t, v)
            y = (g * jax.nn.sigmoid(g)) * (o * _rms_scale(o))
            r.ymix[rows, CONV_DIM + c0:CONV_DIM + c0 + HEAD_DIM] = y.astype(BF16)


def _mixer_kernel(x, mod, pos, gpre, gpost, win, convw, wout, out,
                  h, proj, ymix, mix, state, ucarry, dmat, qdec, kdec, cosr, sinr, cos2, sin2):
    tile_in_seq = pl.program_id(1)
    slot = jnp.minimum(tile_in_seq, 0)
    r = types.SimpleNamespace(
        x=x, mod=mod, pos=pos, gpre=gpre, gpost=gpost, convw=convw, out=out, h=h, proj=proj.at[0],
        proj_conv=proj.at[slot], ymix=ymix,
        mix=mix, state=state, ucarry=ucarry, dmat=dmat, qdec=qdec, kdec=kdec, cosr=cosr, sinr=sinr,
        cos2=cos2, sin2=sin2)

    @pl.when(jnp.logical_and(pl.program_id(0) == 0, tile_in_seq == 0))
    def _():
        _fill_tables(r)
        state[...] = jnp.zeros_like(state)
        ucarry[...] = jnp.zeros_like(ucarry)

    fresh = tile_in_seq == 0
    _norm(r)
    proj[0, :, 0:IN_DIM] = _dot(h[:, 0:D_MODEL], win[...])
    _conv(r, fresh)
    _retention(r, tile_in_seq, fresh)
    mix[:, 0:D_MODEL] = _dot(ymix[:, 0:D_MODEL], wout[...])
    _finish(r)


def _mixer(x, mod, pos, g_pre, g_post, w_in, conv_w, w_out):
    batch, seq, _ = x.shape
    const = dict(pipeline_mode=pl.Buffered(1))
    fixed = lambda b, t: (0, 0)
    tile = (SEQ_TILE, D_MODEL)
    padded = (SEQ_TILE, D_MODEL + LANES)
    return pl.pallas_call(
        _mixer_kernel,
        grid=(batch, seq // SEQ_TILE),
        in_specs=[
            pl.BlockSpec((1,) + tile, lambda b, t: (b, t, 0)),
            pl.BlockSpec((1, N_MOD, D_MODEL), lambda b, t: (b, 0, 0)),
            pl.BlockSpec((1, seq // LANES, LANES), lambda b, t: (b, 0, 0)),
            pl.BlockSpec((1, D_MODEL), fixed),
            pl.BlockSpec((1, D_MODEL), fixed),
            pl.BlockSpec((D_MODEL, IN_DIM), fixed, **const),
            pl.BlockSpec((CONV_WIDTH, CONV_DIM), fixed),
            pl.BlockSpec((D_MODEL, D_MODEL), fixed, **const),
        ],
        out_specs=pl.BlockSpec((1,) + tile, lambda b, t: (b, t, 0)),
        out_shape=jax.ShapeDtypeStruct(x.shape, F32),
        scratch_shapes=[
            pltpu.VMEM(padded, BF16),
            pltpu.VMEM((1, SEQ_TILE, IN_DIM + LANES), F32),
            pltpu.VMEM(padded, BF16),
            pltpu.VMEM(padded, F32),
            pltpu.VMEM((RET_HEADS, HEAD_DIM, HEAD_DIM), F32),
            pltpu.VMEM((SUBLANES, CONV_DIM), F32),
            pltpu.VMEM((RET_HEADS, RET_TILE, RET_TILE), F32),
            pltpu.VMEM((RET_HEADS, RET_TILE, HEAD_DIM), F32),
            pltpu.VMEM((RET_HEADS, RET_TILE, HEAD_DIM), F32),
            pltpu.VMEM((RET_TILE, HEAD_DIM), F32),
            pltpu.VMEM((RET_TILE, HEAD_DIM), F32),
            pltpu.VMEM((RET_TILE, HEAD_DIM), F32),
            pltpu.VMEM((RET_TILE, HEAD_DIM), F32),
        ],
        compiler_params=pltpu.CompilerParams(
            dimension_semantics=("arbitrary", "arbitrary"), vmem_limit_bytes=VMEM_LIMIT),
        name="mixer",
    )(x, mod, pos, g_pre, g_post, w_in, conv_w, w_out)


def _mlp_kernel(x_ref, mod_ref, gpre_ref, gpost_ref, w1_ref, w2_ref, o_ref, act_ref):
    shift = mod_ref[0, 3:4, :]
    scale = mod_ref[0, 4:5, :]
    gate = mod_ref[0, 5:6, :]
    x = x_ref[0]
    h = ((x * _rms_scale(x)) * (gpre_ref[...] * (1.0 + scale)) + shift).astype(BF16)
    for j in range(D_FF // FF_BLOCK):
        a = jnp.maximum(_dot(h, w1_ref[:, j * FF_BLOCK:(j + 1) * FF_BLOCK]), 0.0)
        act_ref[:, j * FF_BLOCK:(j + 1) * FF_BLOCK] = (a * a).astype(BF16)
    f = _dot(act_ref[:, 0:D_FF], w2_ref[...])
    o_ref[0] = x_ref[0] + (f * _rms_scale(f)) * (gate * gpost_ref[...])


def _mlp(x, mod, g_pre, g_post, w1, w2):
    batch, seq, _ = x.shape
    const = dict(pipeline_mode=pl.Buffered(1))
    return pl.pallas_call(
        _mlp_kernel,
        grid=(batch, seq // MLP_TILE),
        in_specs=[
            pl.BlockSpec((1, MLP_TILE, D_MODEL), lambda b, t: (b, t, 0)),
            pl.BlockSpec((1, N_MOD, D_MODEL), lambda b, t: (b, 0, 0)),
            pl.BlockSpec((1, D_MODEL), lambda b, t: (0, 0)),
            pl.BlockSpec((1, D_MODEL), lambda b, t: (0, 0)),
            pl.BlockSpec((D_MODEL, D_FF), lambda b, t: (0, 0), **const),
            pl.BlockSpec((D_FF, D_MODEL), lambda b, t: (0, 0), **const),
        ],
        out_specs=pl.BlockSpec((1, MLP_TILE, D_MODEL), lambda b, t: (b, t, 0)),
        out_shape=jax.ShapeDtypeStruct(x.shape, F32),
        scratch_shapes=[pltpu.VMEM((MLP_TILE, D_FF + LANES), BF16)],
        compiler_params=pltpu.CompilerParams(
            dimension_semantics=("arbitrary", "arbitrary"), vmem_limit_bytes=VMEM_LIMIT),
        name="mlp",
    )(x, mod, g_pre, g_post, w1, w2)


def kernel(x, c, positions, w_ada, b_ada, g_pre_mix, g_post_mix, w_in, conv_w, w_out,
           g_pre_mlp, g_post_mlp, w_fc1, w_fc2):
    batch, seq, d = x.shape
    assert d == D_MODEL and seq % SEQ_TILE == 0 and seq % MLP_TILE == 0
    assert SEQ_TILE % RET_TILE == 0 and RET_TILE % CHUNK == 0 and RET_TILE % LANES == 0
    pos = positions.astype(F32).reshape(batch, seq // LANES, LANES)
    for layer in range(w_ada.shape[0]):
        mod = _ada(c, w_ada[layer], b_ada[layer]).reshape(batch, N_MOD, D_MODEL)
        x = _mixer(x, mod, pos, g_pre_mix[layer][None], g_post_mix[layer][None],
                   w_in[layer].astype(BF16), conv_w[layer], w_out[layer].astype(BF16))
        x = _mlp(x, mod, g_pre_mlp[layer][None], g_post_mlp[layer][None],
                 w_fc1[layer].astype(BF16), w_fc2[layer].astype(BF16))
    return x
```

```python
import math
import types

import jax
import jax.numpy as jnp
from jax import lax
from jax.experimental import pallas as pl
from jax.experimental.pallas import tpu as pltpu

F32 = jnp.float32
BF16 = jnp.bfloat16

D_MODEL = 1024
CHUNK = 64
CONV_WIDTH = 3
CONV_DIM = D_MODEL // 2
RET_HEADS = 4
RET_DIM = D_MODEL - CONV_DIM
HEAD_DIM = RET_DIM // RET_HEADS
IN_DIM = 3 * CONV_DIM + 4 * RET_DIM
D_FF = 4 * D_MODEL
ROPE_BASE = 10000.0
DECAY_OFFSET = 5.0
EPS = 1e-6
N_MOD = 6

LANES = 128
SUBLANES = 8
SEQ_TILE = 1024
RET_TILE = 256
ROW_BLOCK = 32
MLP_TILE = 1024
FF_BLOCK = 512
ADA_BLOCK = 768
VMEM_LIMIT = 58 * 1024 * 1024

_XIN, _BG, _CG = 0, CONV_DIM, 2 * CONV_DIM
_Q = 3 * CONV_DIM
_K = _Q + RET_DIM
_V = _K + RET_DIM
_G = _V + RET_DIM

_LOG_GAMMA = [math.log1p(-2.0 ** (-DECAY_OFFSET - h)) for h in range(RET_HEADS)]


def _dot(a, b):
    return jnp.dot(a, b, preferred_element_type=F32)


def _rms_scale(v):
    return lax.rsqrt(jnp.mean(v * v, axis=-1, keepdims=True) + EPS)


def _ada_kernel(c_ref, w_ref, b_ref, o_ref):
    c = c_ref[...]
    a = (c * jax.nn.sigmoid(c)).astype(BF16)
    o_ref[...] = _dot(a, w_ref[...].astype(BF16)) + b_ref[...]


def _ada(c, w_ada, b_ada):
    batch = c.shape[0]
    n = w_ada.shape[1]
    return pl.pallas_call(
        _ada_kernel,
        grid=(n // ADA_BLOCK,),
        in_specs=[
            pl.BlockSpec((batch, D_MODEL), lambda j: (0, 0)),
            pl.BlockSpec((D_MODEL, ADA_BLOCK), lambda j: (0, j)),
            pl.BlockSpec((1, ADA_BLOCK), lambda j: (0, j)),
        ],
        out_specs=pl.BlockSpec((batch, ADA_BLOCK), lambda j: (0, j)),
        out_shape=jax.ShapeDtypeStruct((batch, n), F32),
        compiler_params=pltpu.CompilerParams(dimension_semantics=("arbitrary",)),
        name="ada_mod",
    )(c, w_ada, b_ada.reshape(1, n))


def _inv_freq():
    half = HEAD_DIM // 2
    lane = lax.broadcasted_iota(jnp.int32, (1, HEAD_DIM), 1)
    return jnp.exp((lane % half).astype(F32) * (-math.log(ROPE_BASE) / half))


def _fill_tables(r):
    k_scale = HEAD_DIM ** -0.5
    i = lax.broadcasted_iota(jnp.int32, (RET_TILE, RET_TILE), 0)
    j = lax.broadcasted_iota(jnp.int32, (RET_TILE, RET_TILE), 1)
    ci = i // CHUNK
    cj = j // CHUNK
    dist = jnp.where(ci == cj, jnp.abs(i - j), i - j).astype(F32)
    t = lax.broadcasted_iota(jnp.int32, (RET_TILE, HEAD_DIM), 0).astype(F32)
    for h in range(RET_HEADS):
        lg = _LOG_GAMMA[h]
        r.dmat[h] = jnp.where(cj <= ci, k_scale * jnp.exp(lg * dist), 0.0)
        r.qdec[h] = jnp.exp(lg * (t + 1.0))
        r.kdec[h] = k_scale * jnp.exp(lg * (RET_TILE - 1.0 - t))
    ang = t * _inv_freq()
    r.cosr[...] = jnp.cos(ang)
    r.sinr[...] = jnp.sin(ang)


def _norm(r):
    gain = r.gpre[...] * (1.0 + r.mod[0, 1:2, :])
    shift = r.mod[0, 0:1, :]
    for r0 in range(0, SEQ_TILE, ROW_BLOCK):
        rows = slice(r0, r0 + ROW_BLOCK)
        x = r.x[0, rows, :]
        r.h[rows, 0:D_MODEL] = ((x * _rms_scale(x)) * gain + shift).astype(BF16)


def _finish(r):
    gain = r.mod[0, 2:3, :] * r.gpost[...]
    for r0 in range(0, SEQ_TILE, ROW_BLOCK):
        rows = slice(r0, r0 + ROW_BLOCK)
        mix = r.mix[rows, 0:D_MODEL]
        r.out[0, rows, :] = r.x[0, rows, :] + (mix * _rms_scale(mix)) * gain


def _conv(r, fresh):
    tail = jnp.where(fresh, 0.0, r.ucarry[...])
    w0, w1, w2 = r.convw[0:1, :], r.convw[1:2, :], r.convw[2:3, :]
    for r0 in range(0, SEQ_TILE, ROW_BLOCK):
        rows = slice(r0, r0 + ROW_BLOCK)
        u = r.proj[rows, _CG:_CG + CONV_DIM] * r.proj[rows, _XIN:_XIN + CONV_DIM]
        ext = jnp.concatenate([tail, u], axis=0)
        u1 = pltpu.roll(ext, 1, axis=0)[SUBLANES:]
        u2 = pltpu.roll(ext, 2, axis=0)[SUBLANES:]
        y_conv = r.proj[rows, _BG:_BG + CONV_DIM] * (w0 * u2 + w1 * u1 + w2 * u)
        r.ymix[rows, 0:CONV_DIM] = y_conv.astype(BF16)
        tail = u[ROW_BLOCK - SUBLANES:]
    r.ucarry[...] = tail


def _retention(r, tile_in_seq, fresh):
    half = HEAD_DIM // 2
    lane = lax.broadcasted_iota(jnp.int32, (1, HEAD_DIM), 1)
    sign = jnp.where(lane < half, -1.0, 1.0)
    inv_freq = _inv_freq()

    for s in range(SEQ_TILE // RET_TILE):
        rows = slice(s * RET_TILE, (s + 1) * RET_TILE)
        p0 = r.pos[0, pl.ds(tile_in_seq * (SEQ_TILE // LANES) + s * (RET_TILE // LANES), 1), 0:1]
        base = p0 * inv_freq
        cos_b = jnp.cos(base)
        sin_b = jnp.sin(base)
        for r0 in range(0, RET_TILE, ROW_BLOCK):
            blk = slice(r0, r0 + ROW_BLOCK)
            cosr = r.cosr[blk, :]
            sinr = r.sinr[blk, :]
            r.cos2[blk, :] = cosr * cos_b - sinr * sin_b
            r.sin2[blk, :] = sinr * (cos_b * sign) + cosr * (sin_b * sign)

        for hd in range(RET_HEADS):
            c0 = hd * HEAD_DIM
            q = r.proj[rows, _Q + c0:_Q + c0 + HEAD_DIM]
            k = r.proj[rows, _K + c0:_K + c0 + HEAD_DIM]
            v = r.proj[rows, _V + c0:_V + c0 + HEAD_DIM].astype(BF16)
            g = r.proj[rows, _G + c0:_G + c0 + HEAD_DIM]
            qr = q * r.cos2[...] + pltpu.roll(q, half, axis=1) * r.sin2[...]
            kr = k * r.cos2[...] + pltpu.roll(k, half, axis=1) * r.sin2[...]
            scores = lax.dot_general(qr.astype(BF16), kr.astype(BF16), (((1,), (1,)), ((), ())),
                                     preferred_element_type=F32)
            p = (scores * r.dmat[hd]).astype(BF16)
            state = r.state[hd]
            if s == 0:
                state = jnp.where(fresh, 0.0, state)
            o = _dot(p, v) + _dot((qr * r.qdec[hd]).astype(BF16), state.astype(BF16))
            kd_t = (kr * r.kdec[hd]).T.astype(BF16)
            r.state[hd] = state * math.exp(_LOG_GAMMA[hd] * RET_TILE) + _dot(kd_t, v)
            y = (g * jax.nn.sigmoid(g)) * (o * _rms_scale(o))
            r.ymix[rows, CONV_DIM + c0:CONV_DIM + c0 + HEAD_DIM] = y.astype(BF16)


def _mixer_kernel(x, mod, pos, gpre, gpost, win, convw, wout, out,
                  h, proj, ymix, mix, state, ucarry, dmat, qdec, kdec, cosr, sinr, cos2, sin2):
    tile_in_seq = pl.program_id(1)
    r = types.SimpleNamespace(
        x=x, mod=mod, pos=pos, gpre=gpre, gpost=gpost, convw=convw, out=out, h=h, proj=proj, ymix=ymix,
        mix=mix, state=state, ucarry=ucarry, dmat=dmat, qdec=qdec, kdec=kdec, cosr=cosr, sinr=sinr,
        cos2=cos2, sin2=sin2)

    @pl.when(jnp.logical_and(pl.program_id(0) == 0, tile_in_seq == 0))
    def _():
        _fill_tables(r)
        state[...] = jnp.zeros_like(state)
        ucarry[...] = jnp.zeros_like(ucarry)

    fresh = tile_in_seq == 0
    _norm(r)
    proj[:, 0:IN_DIM] = _dot(h[:, 0:D_MODEL], win[...])
    _conv(r, fresh)
    _retention(r, tile_in_seq, fresh)
    mix[:, 0:D_MODEL] = _dot(ymix[:, 0:D_MODEL], wout[...])
    _finish(r)


def _mixer(x, mod, pos, g_pre, g_post, w_in, conv_w, w_out):
    batch, seq, _ = x.shape
    const = dict(pipeline_mode=pl.Buffered(1))
    fixed = lambda b, t: (0, 0)
    tile = (SEQ_TILE, D_MODEL)
    padded = (SEQ_TILE, D_MODEL + LANES)
    return pl.pallas_call(
        _mixer_kernel,
        grid=(batch, seq // SEQ_TILE),
        in_specs=[
            pl.BlockSpec((1,) + tile, lambda b, t: (b, t, 0)),
            pl.BlockSpec((1, N_MOD, D_MODEL), lambda b, t: (b, 0, 0)),
            pl.BlockSpec((1, seq // LANES, LANES), lambda b, t: (b, 0, 0)),
            pl.BlockSpec((1, D_MODEL), fixed),
            pl.BlockSpec((1, D_MODEL), fixed),
            pl.BlockSpec((D_MODEL, IN_DIM), fixed, **const),
            pl.BlockSpec((CONV_WIDTH, CONV_DIM), fixed),
            pl.BlockSpec((D_MODEL, D_MODEL), fixed, **const),
        ],
        out_specs=pl.BlockSpec((1,) + tile, lambda b, t: (b, t, 0)),
        out_shape=jax.ShapeDtypeStruct(x.shape, F32),
        scratch_shapes=[
            pltpu.VMEM(padded, BF16),
            pltpu.VMEM((SEQ_TILE, IN_DIM + LANES), F32),
            pltpu.VMEM(padded, BF16),
            pltpu.VMEM(padded, F32),
            pltpu.VMEM((RET_HEADS, HEAD_DIM, HEAD_DIM), F32),
            pltpu.VMEM((SUBLANES, CONV_DIM), F32),
            pltpu.VMEM((RET_HEADS, RET_TILE, RET_TILE), F32),
            pltpu.VMEM((RET_HEADS, RET_TILE, HEAD_DIM), F32),
            pltpu.VMEM((RET_HEADS, RET_TILE, HEAD_DIM), F32),
            pltpu.VMEM((RET_TILE, HEAD_DIM), F32),
            pltpu.VMEM((RET_TILE, HEAD_DIM), F32),
            pltpu.VMEM((RET_TILE, HEAD_DIM), F32),
            pltpu.VMEM((RET_TILE, HEAD_DIM), F32),
        ],
        compiler_params=pltpu.CompilerParams(
            dimension_semantics=("arbitrary", "arbitrary"), vmem_limit_bytes=VMEM_LIMIT),
        name="mixer",
    )(x, mod, pos, g_pre, g_post, w_in, conv_w, w_out)


def _mlp_kernel(x_ref, mod_ref, gpre_ref, gpost_ref, w1_ref, w2_ref, o_ref, act_ref):
    shift = mod_ref[0, 3:4, :]
    scale = mod_ref[0, 4:5, :]
    gate = mod_ref[0, 5:6, :]
    x = x_ref[0]
    h = ((x * _rms_scale(x)) * (gpre_ref[...] * (1.0 + scale)) + shift).astype(BF16)
    for j in range(D_FF // FF_BLOCK):
        a = jnp.maximum(_dot(h, w1_ref[:, j * FF_BLOCK:(j + 1) * FF_BLOCK]), 0.0)
        act_ref[:, j * FF_BLOCK:(j + 1) * FF_BLOCK] = (a * a).astype(BF16)
    f = _dot(act_ref[:, 0:D_FF], w2_ref[...])
    o_ref[0] = x_ref[0] + (f * _rms_scale(f)) * (gate * gpost_ref[...])


def _mlp(x, mod, g_pre, g_post, w1, w2):
    batch, seq, _ = x.shape
    const = dict(pipeline_mode=pl.Buffered(1))
    return pl.pallas_call(
        _mlp_kernel,
        grid=(batch, seq // MLP_TILE),
        in_specs=[
            pl.BlockSpec((1, MLP_TILE, D_MODEL), lambda b, t: (b, t, 0)),
            pl.BlockSpec((1, N_MOD, D_MODEL), lambda b, t: (b, 0, 0)),
            pl.BlockSpec((1, D_MODEL), lambda b, t: (0, 0)),
            pl.BlockSpec((1, D_MODEL), lambda b, t: (0, 0)),
            pl.BlockSpec((D_MODEL, D_FF), lambda b, t: (0, 0), **const),
            pl.BlockSpec((D_FF, D_MODEL), lambda b, t: (0, 0), **const),
        ],
        out_specs=pl.BlockSpec((1, MLP_TILE, D_MODEL), lambda b, t: (b, t, 0)),
        out_shape=jax.ShapeDtypeStruct(x.shape, F32),
        scratch_shapes=[pltpu.VMEM((MLP_TILE, D_FF + LANES), BF16)],
        compiler_params=pltpu.CompilerParams(
            dimension_semantics=("arbitrary", "arbitrary"), vmem_limit_bytes=VMEM_LIMIT),
        name="mlp",
    )(x, mod, g_pre, g_post, w1, w2)


def kernel(x, c, positions, w_ada, b_ada, g_pre_mix, g_post_mix, w_in, conv_w, w_out,
           g_pre_mlp, g_post_mlp, w_fc1, w_fc2):
    batch, seq, d = x.shape
    assert d == D_MODEL and seq % SEQ_TILE == 0 and seq % MLP_TILE == 0
    assert SEQ_TILE % RET_TILE == 0 and RET_TILE % CHUNK == 0 and RET_TILE % LANES == 0
    pos = positions.astype(F32).reshape(batch, seq // LANES, LANES)
    for layer in range(w_ada.shape[0]):
        mod = _ada(c, w_ada[layer], b_ada[layer]).reshape(batch, N_MOD, D_MODEL)
        x = _mixer(x, mod, pos, g_pre_mix[layer][None], g_post_mix[layer][None],
                   w_in[layer].astype(BF16), conv_w[layer], w_out[layer].astype(BF16))
        x = _mlp(x, mod, g_pre_mlp[layer][None], g_post_mlp[layer][None],
                 w_fc1[layer].astype(BF16), w_fc2[layer].astype(BF16))
    return x
```

```python
import math
import types

import jax
import jax.numpy as jnp
from jax import lax
from jax.experimental import pallas as pl
from jax.experimental.pallas import tpu as pltpu

F32 = jnp.float32
BF16 = jnp.bfloat16

D_MODEL = 1024
CHUNK = 64
CONV_WIDTH = 3
CONV_DIM = D_MODEL // 2
RET_HEADS = 4
RET_DIM = D_MODEL - CONV_DIM
HEAD_DIM = RET_DIM // RET_HEADS
IN_DIM = 3 * CONV_DIM + 4 * RET_DIM
D_FF = 4 * D_MODEL
ROPE_BASE = 10000.0
DECAY_OFFSET = 5.0
EPS = 1e-6
N_MOD = 6

LANES = 128
SUBLANES = 8
SEQ_TILE = 1024
RET_TILE = 256
ROW_BLOCK = 32
MLP_TILE = 1024
FF_BLOCK = 512
FC2_ROWS = 256
ADA_BLOCK = 768
VMEM_LIMIT = 58 * 1024 * 1024

_XIN, _BG, _CG = 0, CONV_DIM, 2 * CONV_DIM
_Q = 3 * CONV_DIM
_K = _Q + RET_DIM
_V = _K + RET_DIM
_G = _V + RET_DIM

_LOG_GAMMA = [math.log1p(-2.0 ** (-DECAY_OFFSET - h)) for h in range(RET_HEADS)]


def _dot(a, b):
    return jnp.dot(a, b, preferred_element_type=F32)


def _rms_scale(v):
    return lax.rsqrt(jnp.mean(v * v, axis=-1, keepdims=True) + EPS)


def _ada_kernel(c_ref, w_ref, b_ref, o_ref):
    c = c_ref[...]
    a = (c * jax.nn.sigmoid(c)).astype(BF16)
    o_ref[...] = _dot(a, w_ref[...].astype(BF16)) + b_ref[...]


def _ada(c, w_ada, b_ada):
    batch = c.shape[0]
    n = w_ada.shape[1]
    return pl.pallas_call(
        _ada_kernel,
        grid=(n // ADA_BLOCK,),
        in_specs=[
            pl.BlockSpec((batch, D_MODEL), lambda j: (0, 0)),
            pl.BlockSpec((D_MODEL, ADA_BLOCK), lambda j: (0, j)),
            pl.BlockSpec((1, ADA_BLOCK), lambda j: (0, j)),
        ],
        out_specs=pl.BlockSpec((batch, ADA_BLOCK), lambda j: (0, j)),
        out_shape=jax.ShapeDtypeStruct((batch, n), F32),
        compiler_params=pltpu.CompilerParams(dimension_semantics=("arbitrary",)),
        name="ada_mod",
    )(c, w_ada, b_ada.reshape(1, n))


def _inv_freq():
    half = HEAD_DIM // 2
    lane = lax.broadcasted_iota(jnp.int32, (1, HEAD_DIM), 1)
    return jnp.exp((lane % half).astype(F32) * (-math.log(ROPE_BASE) / half))


def _fill_tables(r):
    k_scale = HEAD_DIM ** -0.5
    i = lax.broadcasted_iota(jnp.int32, (RET_TILE, RET_TILE), 0)
    j = lax.broadcasted_iota(jnp.int32, (RET_TILE, RET_TILE), 1)
    ci = i // CHUNK
    cj = j // CHUNK
    dist = jnp.where(ci == cj, jnp.abs(i - j), i - j).astype(F32)
    t = lax.broadcasted_iota(jnp.int32, (RET_TILE, HEAD_DIM), 0).astype(F32)
    for h in range(RET_HEADS):
        lg = _LOG_GAMMA[h]
        r.dmat[h] = jnp.where(cj <= ci, k_scale * jnp.exp(lg * dist), 0.0)
        r.qdec[h] = jnp.exp(lg * (t + 1.0))
        r.kdec[h] = k_scale * jnp.exp(lg * (RET_TILE - 1.0 - t))
    ang = t * _inv_freq()
    r.cosr[...] = jnp.cos(ang)
    r.sinr[...] = jnp.sin(ang)


def _norm(r):
    gain = r.gpre[...] * (1.0 + r.mod[0, 1:2, :])
    shift = r.mod[0, 0:1, :]
    for r0 in range(0, SEQ_TILE, ROW_BLOCK):
        rows = slice(r0, r0 + ROW_BLOCK)
        x = r.x[0, rows, :]
        r.h[rows, 0:D_MODEL] = ((x * _rms_scale(x)) * gain + shift).astype(BF16)


def _finish(r):
    gain = r.mod[0, 2:3, :] * r.gpost[...]
    for r0 in range(0, SEQ_TILE, ROW_BLOCK):
        rows = slice(r0, r0 + ROW_BLOCK)
        mix = r.mix[rows, 0:D_MODEL]
        r.out[0, rows, :] = r.x[0, rows, :] + (mix * _rms_scale(mix)) * gain


def _conv(r, fresh):
    tail = jnp.where(fresh, 0.0, r.ucarry[...])
    w0, w1, w2 = r.convw[0:1, :], r.convw[1:2, :], r.convw[2:3, :]
    for r0 in range(0, SEQ_TILE, ROW_BLOCK):
        rows = slice(r0, r0 + ROW_BLOCK)
        u = r.proj[rows, _CG:_CG + CONV_DIM] * r.proj[rows, _XIN:_XIN + CONV_DIM]
        ext = jnp.concatenate([tail, u], axis=0)
        u1 = pltpu.roll(ext, 1, axis=0)[SUBLANES:]
        u2 = pltpu.roll(ext, 2, axis=0)[SUBLANES:]
        y_conv = r.proj[rows, _BG:_BG + CONV_DIM] * (w0 * u2 + w1 * u1 + w2 * u)
        r.ymix[rows, 0:CONV_DIM] = y_conv.astype(BF16)
        tail = u[ROW_BLOCK - SUBLANES:]
    r.ucarry[...] = tail


def _retention(r, tile_in_seq, fresh):
    half = HEAD_DIM // 2
    lane = lax.broadcasted_iota(jnp.int32, (1, HEAD_DIM), 1)
    sign = jnp.where(lane < half, -1.0, 1.0)
    inv_freq = _inv_freq()

    for s in range(SEQ_TILE // RET_TILE):
        rows = slice(s * RET_TILE, (s + 1) * RET_TILE)
        p0 = r.pos[0, pl.ds(tile_in_seq * (SEQ_TILE // LANES) + s * (RET_TILE // LANES), 1), 0:1]
        base = p0 * inv_freq
        cos_b = jnp.cos(base)
        sin_b = jnp.sin(base)
        for r0 in range(0, RET_TILE, ROW_BLOCK):
            blk = slice(r0, r0 + ROW_BLOCK)
            cosr = r.cosr[blk, :]
            sinr = r.sinr[blk, :]
            r.cos2[blk, :] = cosr * cos_b - sinr * sin_b
            r.sin2[blk, :] = sinr * (cos_b * sign) + cosr * (sin_b * sign)

        for hd in range(RET_HEADS):
            c0 = hd * HEAD_DIM
            q = r.proj[rows, _Q + c0:_Q + c0 + HEAD_DIM]
            k = r.proj[rows, _K + c0:_K + c0 + HEAD_DIM]
            v = r.proj[rows, _V + c0:_V + c0 + HEAD_DIM].astype(BF16)
            g = r.proj[rows, _G + c0:_G + c0 + HEAD_DIM]
            qr = q * r.cos2[...] + pltpu.roll(q, half, axis=1) * r.sin2[...]
            kr = k * r.cos2[...] + pltpu.roll(k, half, axis=1) * r.sin2[...]
            scores = lax.dot_general(qr.astype(BF16), kr.astype(BF16), (((1,), (1,)), ((), ())),
                                     preferred_element_type=F32)
            p = (scores * r.dmat[hd]).astype(BF16)
            state = r.state[hd]
            if s == 0:
                state = jnp.where(fresh, 0.0, state)
            o = _dot(p, v) + _dot((qr * r.qdec[hd]).astype(BF16), state.astype(BF16))
            kd_t = (kr * r.kdec[hd]).T.astype(BF16)
            r.state[hd] = state * math.exp(_LOG_GAMMA[hd] * RET_TILE) + _dot(kd_t, v)
            y = (g * jax.nn.sigmoid(g)) * (o * _rms_scale(o))
            r.ymix[rows, CONV_DIM + c0:CONV_DIM + c0 + HEAD_DIM] = y.astype(BF16)


def _mixer_kernel(x, mod, pos, gpre, gpost, win, convw, wout, out,
                  h, proj, ymix, mix, state, ucarry, dmat, qdec, kdec, cosr, sinr, cos2, sin2):
    tile_in_seq = pl.program_id(1)
    r = types.SimpleNamespace(
        x=x, mod=mod, pos=pos, gpre=gpre, gpost=gpost, convw=convw, out=out, h=h, proj=proj, ymix=ymix,
        mix=mix, state=state, ucarry=ucarry, dmat=dmat, qdec=qdec, kdec=kdec, cosr=cosr, sinr=sinr,
        cos2=cos2, sin2=sin2)

    @pl.when(jnp.logical_and(pl.program_id(0) == 0, tile_in_seq == 0))
    def _():
        _fill_tables(r)
        state[...] = jnp.zeros_like(state)
        ucarry[...] = jnp.zeros_like(ucarry)

    fresh = tile_in_seq == 0
    _norm(r)
    proj[:, 0:IN_DIM] = _dot(h[:, 0:D_MODEL], win[...])
    _conv(r, fresh)
    _retention(r, tile_in_seq, fresh)
    mix[:, 0:D_MODEL] = _dot(ymix[:, 0:D_MODEL], wout[...])
    _finish(r)


def _mixer(x, mod, pos, g_pre, g_post, w_in, conv_w, w_out):
    batch, seq, _ = x.shape
    const = dict(pipeline_mode=pl.Buffered(1))
    fixed = lambda b, t: (0, 0)
    tile = (SEQ_TILE, D_MODEL)
    padded = (SEQ_TILE, D_MODEL + LANES)
    return pl.pallas_call(
        _mixer_kernel,
        grid=(batch, seq // SEQ_TILE),
        in_specs=[
            pl.BlockSpec((1,) + tile, lambda b, t: (b, t, 0)),
            pl.BlockSpec((1, N_MOD, D_MODEL), lambda b, t: (b, 0, 0)),
            pl.BlockSpec((1, seq // LANES, LANES), lambda b, t: (b, 0, 0)),
            pl.BlockSpec((1, D_MODEL), fixed),
            pl.BlockSpec((1, D_MODEL), fixed),
            pl.BlockSpec((D_MODEL, IN_DIM), fixed, **const),
            pl.BlockSpec((CONV_WIDTH, CONV_DIM), fixed),
            pl.BlockSpec((D_MODEL, D_MODEL), fixed, **const),
        ],
        out_specs=pl.BlockSpec((1,) + tile, lambda b, t: (b, t, 0)),
        out_shape=jax.ShapeDtypeStruct(x.shape, F32),
        scratch_shapes=[
            pltpu.VMEM(padded, BF16),
            pltpu.VMEM((SEQ_TILE, IN_DIM + LANES), F32),
            pltpu.VMEM(padded, BF16),
            pltpu.VMEM(padded, F32),
            pltpu.VMEM((RET_HEADS, HEAD_DIM, HEAD_DIM), F32),
            pltpu.VMEM((SUBLANES, CONV_DIM), F32),
            pltpu.VMEM((RET_HEADS, RET_TILE, RET_TILE), F32),
            pltpu.VMEM((RET_HEADS, RET_TILE, HEAD_DIM), F32),
            pltpu.VMEM((RET_HEADS, RET_TILE, HEAD_DIM), F32),
            pltpu.VMEM((RET_TILE, HEAD_DIM), F32),
            pltpu.VMEM((RET_TILE, HEAD_DIM), F32),
            pltpu.VMEM((RET_TILE, HEAD_DIM), F32),
            pltpu.VMEM((RET_TILE, HEAD_DIM), F32),
        ],
        compiler_params=pltpu.CompilerParams(
            dimension_semantics=("arbitrary", "arbitrary"), vmem_limit_bytes=VMEM_LIMIT),
        name="mixer",
    )(x, mod, pos, g_pre, g_post, w_in, conv_w, w_out)


def _mlp_kernel(x_ref, mod_ref, gpre_ref, gpost_ref, w1_ref, w2_ref, o_ref, act_ref):
    shift = mod_ref[0, 3:4, :]
    scale = mod_ref[0, 4:5, :]
    gate = mod_ref[0, 5:6, :]
    x = x_ref[0]
    h = ((x * _rms_scale(x)) * (gpre_ref[...] * (1.0 + scale)) + shift).astype(BF16)
    for j in range(D_FF // FF_BLOCK):
        a = jnp.maximum(_dot(h, w1_ref[:, j * FF_BLOCK:(j + 1) * FF_BLOCK]), 0.0)
        act_ref[:, j * FF_BLOCK:(j + 1) * FF_BLOCK] = (a * a).astype(BF16)
    gain = gate * gpost_ref[...]
    for r0 in range(0, MLP_TILE, FC2_ROWS):
        rows = slice(r0, r0 + FC2_ROWS)
        f = _dot(act_ref[rows, 0:D_FF], w2_ref[...])
        o_ref[0, rows, :] = x_ref[0, rows, :] + (f * _rms_scale(f)) * gain


def _mlp(x, mod, g_pre, g_post, w1, w2):
    batch, seq, _ = x.shape
    const = dict(pipeline_mode=pl.Buffered(1))
    return pl.pallas_call(
        _mlp_kernel,
        grid=(batch, seq // MLP_TILE),
        in_specs=[
            pl.BlockSpec((1, MLP_TILE, D_MODEL), lambda b, t: (b, t, 0)),
            pl.BlockSpec((1, N_MOD, D_MODEL), lambda b, t: (b, 0, 0)),
            pl.BlockSpec((1, D_MODEL), lambda b, t: (0, 0)),
            pl.BlockSpec((1, D_MODEL), lambda b, t: (0, 0)),
            pl.BlockSpec((D_MODEL, D_FF), lambda b, t: (0, 0), **const),
            pl.BlockSpec((D_FF, D_MODEL), lambda b, t: (0, 0), **const),
        ],
        out_specs=pl.BlockSpec((1, MLP_TILE, D_MODEL), lambda b, t: (b, t, 0)),
        out_shape=jax.ShapeDtypeStruct(x.shape, F32),
        scratch_shapes=[pltpu.VMEM((MLP_TILE, D_FF + LANES), BF16)],
        compiler_params=pltpu.CompilerParams(
            dimension_semantics=("arbitrary", "arbitrary"), vmem_limit_bytes=VMEM_LIMIT),
        name="mlp",
    )(x, mod, g_pre, g_post, w1, w2)


def kernel(x, c, positions, w_ada, b_ada, g_pre_mix, g_post_mix, w_in, conv_w, w_out,
           g_pre_mlp, g_post_mlp, w_fc1, w_fc2):
    batch, seq, d = x.shape
    assert d == D_MODEL and seq % SEQ_TILE == 0 and seq % MLP_TILE == 0
    assert SEQ_TILE % RET_TILE == 0 and RET_TILE % CHUNK == 0 and RET_TILE % LANES == 0
    pos = positions.astype(F32).reshape(batch, seq // LANES, LANES)
    for layer in range(w_ada.shape[0]):
        mod = _ada(c, w_ada[layer], b_ada[layer]).reshape(batch, N_MOD, D_MODEL)
        x = _mixer(x, mod, pos, g_pre_mix[layer][None], g_post_mix[layer][None],
                   w_in[layer].astype(BF16), conv_w[layer], w_out[layer].astype(BF16))
        x = _mlp(x, mod, g_pre_mlp[layer][None], g_post_mlp[layer][None],
                 w_fc1[layer].astype(BF16), w_fc2[layer].astype(BF16))
    return x
```

```python
import math
import types

import jax
import jax.numpy as jnp
from jax import lax
from jax.experimental import pallas as pl
from jax.experimental.pallas import tpu as pltpu

F32 = jnp.float32
BF16 = jnp.bfloat16

D_MODEL = 1024
CHUNK = 64
CONV_WIDTH = 3
CONV_DIM = D_MODEL // 2
RET_HEADS = 4
RET_DIM = D_MODEL - CONV_DIM
HEAD_DIM = RET_DIM // RET_HEADS
IN_DIM = 3 * CONV_DIM + 4 * RET_DIM
D_FF = 4 * D_MODEL
ROPE_BASE = 10000.0
DECAY_OFFSET = 5.0
EPS = 1e-6
N_MOD = 6

LANES = 128
SUBLANES = 8
SEQ_TILE = 1024
RET_TILE = 256
ROW_BLOCK = 32
MLP_TILE = 1024
FF_BLOCK = 512
ADA_BLOCK = 768
W_CHUNK_ROWS = 128
VMEM_LIMIT = 58 * 1024 * 1024

_XIN, _BG, _CG = 0, CONV_DIM, 2 * CONV_DIM
_Q = 3 * CONV_DIM
_K = _Q + RET_DIM
_V = _K + RET_DIM
_G = _V + RET_DIM

_LOG_GAMMA = [math.log1p(-2.0 ** (-DECAY_OFFSET - h)) for h in range(RET_HEADS)]


def _dot(a, b):
    return jnp.dot(a, b, preferred_element_type=F32)


def _rms_scale(v):
    return lax.rsqrt(jnp.mean(v * v, axis=-1, keepdims=True) + EPS)


def _load_weight_bf16(w_hbm, w_vmem, stage, sem):
    chunk = stage.shape[1]
    n_chunks = w_hbm.shape[0] // chunk

    def copy(k):
        return pltpu.make_async_copy(w_hbm.at[pl.ds(k * chunk, chunk), :], stage.at[k % 2], sem.at[k % 2])

    copy(0).start()
    for k in range(n_chunks):
        if k + 1 < n_chunks:
            copy(k + 1).start()
        copy(k).wait()
        w_vmem[k * chunk:(k + 1) * chunk, :] = stage[k % 2].astype(BF16)


def _ada_kernel(c_ref, w_ref, b_ref, o_ref):
    c = c_ref[...]
    a = (c * jax.nn.sigmoid(c)).astype(BF16)
    o_ref[...] = _dot(a, w_ref[...].astype(BF16)) + b_ref[...]


def _ada(c, w_ada, b_ada):
    batch = c.shape[0]
    n = w_ada.shape[1]
    return pl.pallas_call(
        _ada_kernel,
        grid=(n // ADA_BLOCK,),
        in_specs=[
            pl.BlockSpec((batch, D_MODEL), lambda j: (0, 0)),
            pl.BlockSpec((D_MODEL, ADA_BLOCK), lambda j: (0, j)),
            pl.BlockSpec((1, ADA_BLOCK), lambda j: (0, j)),
        ],
        out_specs=pl.BlockSpec((batch, ADA_BLOCK), lambda j: (0, j)),
        out_shape=jax.ShapeDtypeStruct((batch, n), F32),
        compiler_params=pltpu.CompilerParams(dimension_semantics=("arbitrary",)),
        name="ada_mod",
    )(c, w_ada, b_ada.reshape(1, n))


def _inv_freq():
    half = HEAD_DIM // 2
    lane = lax.broadcasted_iota(jnp.int32, (1, HEAD_DIM), 1)
    return jnp.exp((lane % half).astype(F32) * (-math.log(ROPE_BASE) / half))


def _fill_tables(r):
    k_scale = HEAD_DIM ** -0.5
    i = lax.broadcasted_iota(jnp.int32, (RET_TILE, RET_TILE), 0)
    j = lax.broadcasted_iota(jnp.int32, (RET_TILE, RET_TILE), 1)
    ci = i // CHUNK
    cj = j // CHUNK
    dist = jnp.where(ci == cj, jnp.abs(i - j), i - j).astype(F32)
    t = lax.broadcasted_iota(jnp.int32, (RET_TILE, HEAD_DIM), 0).astype(F32)
    for h in range(RET_HEADS):
        lg = _LOG_GAMMA[h]
        r.dmat[h] = jnp.where(cj <= ci, k_scale * jnp.exp(lg * dist), 0.0)
        r.qdec[h] = jnp.exp(lg * (t + 1.0))
        r.kdec[h] = k_scale * jnp.exp(lg * (RET_TILE - 1.0 - t))
    ang = t * _inv_freq()
    r.cosr[...] = jnp.cos(ang)
    r.sinr[...] = jnp.sin(ang)


def _norm(r):
    gain = r.gpre[...] * (1.0 + r.mod[0, 1:2, :])
    shift = r.mod[0, 0:1, :]
    for r0 in range(0, SEQ_TILE, ROW_BLOCK):
        rows = slice(r0, r0 + ROW_BLOCK)
        x = r.x[0, rows, :]
        r.h[rows, 0:D_MODEL] = ((x * _rms_scale(x)) * gain + shift).astype(BF16)


def _finish(r):
    gain = r.mod[0, 2:3, :] * r.gpost[...]
    for r0 in range(0, SEQ_TILE, ROW_BLOCK):
        rows = slice(r0, r0 + ROW_BLOCK)
        mix = r.mix[rows, 0:D_MODEL]
        r.out[0, rows, :] = r.x[0, rows, :] + (mix * _rms_scale(mix)) * gain


def _conv(r, fresh):
    tail = jnp.where(fresh, 0.0, r.ucarry[...])
    w0, w1, w2 = r.convw[0:1, :], r.convw[1:2, :], r.convw[2:3, :]
    for r0 in range(0, SEQ_TILE, ROW_BLOCK):
        rows = slice(r0, r0 + ROW_BLOCK)
        u = r.proj[rows, _CG:_CG + CONV_DIM] * r.proj[rows, _XIN:_XIN + CONV_DIM]
        ext = jnp.concatenate([tail, u], axis=0)
        u1 = pltpu.roll(ext, 1, axis=0)[SUBLANES:]
        u2 = pltpu.roll(ext, 2, axis=0)[SUBLANES:]
        y_conv = r.proj[rows, _BG:_BG + CONV_DIM] * (w0 * u2 + w1 * u1 + w2 * u)
        r.ymix[rows, 0:CONV_DIM] = y_conv.astype(BF16)
        tail = u[ROW_BLOCK - SUBLANES:]
    r.ucarry[...] = tail


def _retention(r, tile_in_seq, fresh):
    half = HEAD_DIM // 2
    lane = lax.broadcasted_iota(jnp.int32, (1, HEAD_DIM), 1)
    sign = jnp.where(lane < half, -1.0, 1.0)
    inv_freq = _inv_freq()

    for s in range(SEQ_TILE // RET_TILE):
        rows = slice(s * RET_TILE, (s + 1) * RET_TILE)
        p0 = r.pos[0, pl.ds(tile_in_seq * (SEQ_TILE // LANES) + s * (RET_TILE // LANES), 1), 0:1]
        base = p0 * inv_freq
        cos_b = jnp.cos(base)
        sin_b = jnp.sin(base)
        for r0 in range(0, RET_TILE, ROW_BLOCK):
            blk = slice(r0, r0 + ROW_BLOCK)
            cosr = r.cosr[blk, :]
            sinr = r.sinr[blk, :]
            r.cos2[blk, :] = cosr * cos_b - sinr * sin_b
            r.sin2[blk, :] = sinr * (cos_b * sign) + cosr * (sin_b * sign)

        for hd in range(RET_HEADS):
            c0 = hd * HEAD_DIM
            q = r.proj[rows, _Q + c0:_Q + c0 + HEAD_DIM]
            k = r.proj[rows, _K + c0:_K + c0 + HEAD_DIM]
            v = r.proj[rows, _V + c0:_V + c0 + HEAD_DIM].astype(BF16)
            g = r.proj[rows, _G + c0:_G + c0 + HEAD_DIM]
            qr = q * r.cos2[...] + pltpu.roll(q, half, axis=1) * r.sin2[...]
            kr = k * r.cos2[...] + pltpu.roll(k, half, axis=1) * r.sin2[...]
            scores = lax.dot_general(qr.astype(BF16), kr.astype(BF16), (((1,), (1,)), ((), ())),
                                     preferred_element_type=F32)
            p = (scores * r.dmat[hd]).astype(BF16)
            state = r.state[hd]
            if s == 0:
                state = jnp.where(fresh, 0.0, state)
            o = _dot(p, v) + _dot((qr * r.qdec[hd]).astype(BF16), state.astype(BF16))
            kd_t = (kr * r.kdec[hd]).T.astype(BF16)
            r.state[hd] = state * math.exp(_LOG_GAMMA[hd] * RET_TILE) + _dot(kd_t, v)
            y = (g * jax.nn.sigmoid(g)) * (o * _rms_scale(o))
            r.ymix[rows, CONV_DIM + c0:CONV_DIM + c0 + HEAD_DIM] = y.astype(BF16)


def _mixer_kernel(x, mod, pos, gpre, gpost, win_hbm, convw, wout_hbm, out,
                  win, wout, stage_in, stage_out, sem_in, sem_out,
                  h, proj, ymix, mix, state, ucarry, dmat, qdec, kdec, cosr, sinr, cos2, sin2):
    tile_in_seq = pl.program_id(1)
    r = types.SimpleNamespace(
        x=x, mod=mod, pos=pos, gpre=gpre, gpost=gpost, convw=convw, out=out, h=h, proj=proj, ymix=ymix,
        mix=mix, state=state, ucarry=ucarry, dmat=dmat, qdec=qdec, kdec=kdec, cosr=cosr, sinr=sinr,
        cos2=cos2, sin2=sin2)

    @pl.when(jnp.logical_and(pl.program_id(0) == 0, tile_in_seq == 0))
    def _():
        _load_weight_bf16(win_hbm, win, stage_in, sem_in)
        _load_weight_bf16(wout_hbm, wout, stage_out, sem_out)
        _fill_tables(r)
        state[...] = jnp.zeros_like(state)
        ucarry[...] = jnp.zeros_like(ucarry)

    fresh = tile_in_seq == 0
    _norm(r)
    proj[:, 0:IN_DIM] = _dot(h[:, 0:D_MODEL], win[...])
    _conv(r, fresh)
    _retention(r, tile_in_seq, fresh)
    mix[:, 0:D_MODEL] = _dot(ymix[:, 0:D_MODEL], wout[...])
    _finish(r)


def _mixer(x, mod, pos, g_pre, g_post, w_in, conv_w, w_out):
    batch, seq, _ = x.shape
    fixed = lambda b, t: (0, 0)
    tile = (SEQ_TILE, D_MODEL)
    padded = (SEQ_TILE, D_MODEL + LANES)
    return pl.pallas_call(
        _mixer_kernel,
        grid=(batch, seq // SEQ_TILE),
        in_specs=[
            pl.BlockSpec((1,) + tile, lambda b, t: (b, t, 0)),
            pl.BlockSpec((1, N_MOD, D_MODEL), lambda b, t: (b, 0, 0)),
            pl.BlockSpec((1, seq // LANES, LANES), lambda b, t: (b, 0, 0)),
            pl.BlockSpec((1, D_MODEL), fixed),
            pl.BlockSpec((1, D_MODEL), fixed),
            pl.BlockSpec(memory_space=pl.ANY),
            pl.BlockSpec((CONV_WIDTH, CONV_DIM), fixed),
            pl.BlockSpec(memory_space=pl.ANY),
        ],
        out_specs=pl.BlockSpec((1,) + tile, lambda b, t: (b, t, 0)),
        out_shape=jax.ShapeDtypeStruct(x.shape, F32),
        scratch_shapes=[
            pltpu.VMEM((D_MODEL, IN_DIM), BF16),
            pltpu.VMEM((D_MODEL, D_MODEL), BF16),
            pltpu.VMEM((2, W_CHUNK_ROWS, IN_DIM), F32),
            pltpu.VMEM((2, W_CHUNK_ROWS, D_MODEL), F32),
            pltpu.SemaphoreType.DMA((2,)),
            pltpu.SemaphoreType.DMA((2,)),
            pltpu.VMEM(padded, BF16),
            pltpu.VMEM((SEQ_TILE, IN_DIM + LANES), F32),
            pltpu.VMEM(padded, BF16),
            pltpu.VMEM(padded, F32),
            pltpu.VMEM((RET_HEADS, HEAD_DIM, HEAD_DIM), F32),
            pltpu.VMEM((SUBLANES, CONV_DIM), F32),
            pltpu.VMEM((RET_HEADS, RET_TILE, RET_TILE), F32),
            pltpu.VMEM((RET_HEADS, RET_TILE, HEAD_DIM), F32),
            pltpu.VMEM((RET_HEADS, RET_TILE, HEAD_DIM), F32),
            pltpu.VMEM((RET_TILE, HEAD_DIM), F32),
            pltpu.VMEM((RET_TILE, HEAD_DIM), F32),
            pltpu.VMEM((RET_TILE, HEAD_DIM), F32),
            pltpu.VMEM((RET_TILE, HEAD_DIM), F32),
        ],
        compiler_params=pltpu.CompilerParams(
            dimension_semantics=("arbitrary", "arbitrary"), vmem_limit_bytes=VMEM_LIMIT),
        name="mixer",
    )(x, mod, pos, g_pre, g_post, w_in, conv_w, w_out)


def _mlp_kernel(x_ref, mod_ref, gpre_ref, gpost_ref, w1_hbm, w2_hbm, o_ref,
                w1_ref, w2_ref, stage1, stage2, sem1, sem2, act_ref):
    @pl.when(jnp.logical_and(pl.program_id(0) == 0, pl.program_id(1) == 0))
    def _():
        _load_weight_bf16(w1_hbm, w1_ref, stage1, sem1)
        _load_weight_bf16(w2_hbm, w2_ref, stage2, sem2)

    shift = mod_ref[0, 3:4, :]
    scale = mod_ref[0, 4:5, :]
    gate = mod_ref[0, 5:6, :]
    x = x_ref[0]
    h = ((x * _rms_scale(x)) * (gpre_ref[...] * (1.0 + scale)) + shift).astype(BF16)
    for j in range(D_FF // FF_BLOCK):
        a = jnp.maximum(_dot(h, w1_ref[:, j * FF_BLOCK:(j + 1) * FF_BLOCK]), 0.0)
        act_ref[:, j * FF_BLOCK:(j + 1) * FF_BLOCK] = (a * a).astype(BF16)
    f = _dot(act_ref[:, 0:D_FF], w2_ref[...])
    o_ref[0] = x_ref[0] + (f * _rms_scale(f)) * (gate * gpost_ref[...])


def _mlp(x, mod, g_pre, g_post, w1, w2):
    batch, seq, _ = x.shape
    return pl.pallas_call(
        _mlp_kernel,
        grid=(batch, seq // MLP_TILE),
        in_specs=[
            pl.BlockSpec((1, MLP_TILE, D_MODEL), lambda b, t: (b, t, 0)),
            pl.BlockSpec((1, N_MOD, D_MODEL), lambda b, t: (b, 0, 0)),
            pl.BlockSpec((1, D_MODEL), lambda b, t: (0, 0)),
            pl.BlockSpec((1, D_MODEL), lambda b, t: (0, 0)),
            pl.BlockSpec(memory_space=pl.ANY),
            pl.BlockSpec(memory_space=pl.ANY),
        ],
        out_specs=pl.BlockSpec((1, MLP_TILE, D_MODEL), lambda b, t: (b, t, 0)),
        out_shape=jax.ShapeDtypeStruct(x.shape, F32),
        scratch_shapes=[
            pltpu.VMEM((D_MODEL, D_FF), BF16),
            pltpu.VMEM((D_FF, D_MODEL), BF16),
            pltpu.VMEM((2, W_CHUNK_ROWS, D_FF), F32),
            pltpu.VMEM((2, W_CHUNK_ROWS, D_MODEL), F32),
            pltpu.SemaphoreType.DMA((2,)),
            pltpu.SemaphoreType.DMA((2,)),
            pltpu.VMEM((MLP_TILE, D_FF + LANES), BF16),
        ],
        compiler_params=pltpu.CompilerParams(
            dimension_semantics=("arbitrary", "arbitrary"), vmem_limit_bytes=VMEM_LIMIT),
        name="mlp",
    )(x, mod, g_pre, g_post, w1, w2)


def kernel(x, c, positions, w_ada, b_ada, g_pre_mix, g_post_mix, w_in, conv_w, w_out,
           g_pre_mlp, g_post_mlp, w_fc1, w_fc2):
    batch, seq, d = x.shape
    assert d == D_MODEL and seq % SEQ_TILE == 0 and seq % MLP_TILE == 0
    assert SEQ_TILE % RET_TILE == 0 and RET_TILE % CHUNK == 0 and RET_TILE % LANES == 0
    pos = positions.astype(F32).reshape(batch, seq // LANES, LANES)
    for layer in range(w_ada.shape[0]):
        mod = _ada(c, w_ada[layer], b_ada[layer]).reshape(batch, N_MOD, D_MODEL)
        x = _mixer(x, mod, pos, g_pre_mix[layer][None], g_post_mix[layer][None],
                   w_in[layer], conv_w[layer], w_out[layer])
        x = _mlp(x, mod, g_pre_mlp[layer][None], g_post_mlp[layer][None], w_fc1[layer], w_fc2[layer])
    return x
```

```python
import math
import types

import jax
import jax.numpy as jnp
from jax import lax
from jax.experimental import pallas as pl
from jax.experimental.pallas import tpu as pltpu

F32 = jnp.float32
BF16 = jnp.bfloat16

D_MODEL = 1024
CHUNK = 64
CONV_WIDTH = 3
CONV_DIM = D_MODEL // 2
RET_HEADS = 4
RET_DIM = D_MODEL - CONV_DIM
HEAD_DIM = RET_DIM // RET_HEADS
IN_DIM = 3 * CONV_DIM + 4 * RET_DIM
D_FF = 4 * D_MODEL
ROPE_BASE = 10000.0
DECAY_OFFSET = 5.0
EPS = 1e-6
N_MOD = 6

LANES = 128
SUBLANES = 8
SEQ_TILE = 1024
RET_TILE = 256
ROW_BLOCK = 32
MLP_TILE = 1024
FF_BLOCK = 512
ADA_BLOCK = 2048
W_SLOTS = 4
W_CHUNK_BYTES = 1 << 20


def _stage_shape(cols):
    rows = 1 << int(math.log2(W_CHUNK_BYTES // (4 * cols)))
    return (W_SLOTS, rows, cols)
VMEM_LIMIT = 58 * 1024 * 1024

_XIN, _BG, _CG = 0, CONV_DIM, 2 * CONV_DIM
_Q = 3 * CONV_DIM
_K = _Q + RET_DIM
_V = _K + RET_DIM
_G = _V + RET_DIM

_LOG_GAMMA = [math.log1p(-2.0 ** (-DECAY_OFFSET - h)) for h in range(RET_HEADS)]


def _dot(a, b):
    return jnp.dot(a, b, preferred_element_type=F32)


def _rms_scale(v):
    return lax.rsqrt(jnp.mean(v * v, axis=-1, keepdims=True) + EPS)


def _load_weight_bf16(w_hbm, w_vmem, stage, sem):
    slots, chunk = stage.shape[0], stage.shape[1]
    n_chunks = w_hbm.shape[0] // chunk

    def copy(k):
        return pltpu.make_async_copy(
            w_hbm.at[pl.ds(k * chunk, chunk), :], stage.at[k % slots], sem.at[k % slots])

    for k in range(min(slots - 1, n_chunks)):
        copy(k).start()
    for k in range(n_chunks):
        if k + slots - 1 < n_chunks:
            copy(k + slots - 1).start()
        copy(k).wait()
        w_vmem[k * chunk:(k + 1) * chunk, :] = stage[k % slots].astype(BF16)


def _ada_kernel(c_ref, w_ref, b_ref, o_ref):
    c = c_ref[...]
    a = (c * jax.nn.sigmoid(c)).astype(BF16)
    mod = _dot(a, w_ref[...].astype(BF16)) + b_ref[...]
    for m in range(ADA_BLOCK // D_MODEL):
        o_ref[m] = mod[:, m * D_MODEL:(m + 1) * D_MODEL]


def _ada(c, w_ada, b_ada):
    batch = c.shape[0]
    n = w_ada.shape[1]
    per_step = ADA_BLOCK // D_MODEL
    return pl.pallas_call(
        _ada_kernel,
        grid=(n // ADA_BLOCK,),
        in_specs=[
            pl.BlockSpec((batch, D_MODEL), lambda j: (0, 0)),
            pl.BlockSpec((D_MODEL, ADA_BLOCK), lambda j: (0, j)),
            pl.BlockSpec((1, ADA_BLOCK), lambda j: (0, j)),
        ],
        out_specs=pl.BlockSpec((per_step, batch, D_MODEL), lambda j: (j, 0, 0)),
        out_shape=jax.ShapeDtypeStruct((n // D_MODEL, batch, D_MODEL), F32),
        compiler_params=pltpu.CompilerParams(dimension_semantics=("arbitrary",)),
        name="ada_mod",
    )(c, w_ada, b_ada.reshape(1, n))


def _mod_row(mod_ref, k):
    return mod_ref[k, pl.ds(pl.program_id(0), 1), :]


def _inv_freq():
    half = HEAD_DIM // 2
    lane = lax.broadcasted_iota(jnp.int32, (1, HEAD_DIM), 1)
    return jnp.exp((lane % half).astype(F32) * (-math.log(ROPE_BASE) / half))


def _fill_tables(r):
    k_scale = HEAD_DIM ** -0.5
    i = lax.broadcasted_iota(jnp.int32, (RET_TILE, RET_TILE), 0)
    j = lax.broadcasted_iota(jnp.int32, (RET_TILE, RET_TILE), 1)
    ci = i // CHUNK
    cj = j // CHUNK
    dist = jnp.where(ci == cj, jnp.abs(i - j), i - j).astype(F32)
    t = lax.broadcasted_iota(jnp.int32, (RET_TILE, HEAD_DIM), 0).astype(F32)
    for h in range(RET_HEADS):
        lg = _LOG_GAMMA[h]
        r.dmat[h] = jnp.where(cj <= ci, k_scale * jnp.exp(lg * dist), 0.0)
        r.qdec[h] = jnp.exp(lg * (t + 1.0))
        r.kdec[h] = k_scale * jnp.exp(lg * (RET_TILE - 1.0 - t))
    ang = t * _inv_freq()
    r.cosr[...] = jnp.cos(ang)
    r.sinr[...] = jnp.sin(ang)


def _norm(r):
    gain = r.gpre[...] * (1.0 + _mod_row(r.mod, 1))
    shift = _mod_row(r.mod, 0)
    for r0 in range(0, SEQ_TILE, ROW_BLOCK):
        rows = slice(r0, r0 + ROW_BLOCK)
        x = r.x[0, rows, :]
        r.h[rows, 0:D_MODEL] = ((x * _rms_scale(x)) * gain + shift).astype(BF16)


def _finish(r):
    gain = _mod_row(r.mod, 2) * r.gpost[...]
    for r0 in range(0, SEQ_TILE, ROW_BLOCK):
        rows = slice(r0, r0 + ROW_BLOCK)
        mix = r.mix[rows, 0:D_MODEL]
        r.out[0, rows, :] = r.x[0, rows, :] + (mix * _rms_scale(mix)) * gain


def _conv(r, fresh):
    tail = jnp.where(fresh, 0.0, r.ucarry[...])
    w0, w1, w2 = r.convw[0:1, :], r.convw[1:2, :], r.convw[2:3, :]
    for r0 in range(0, SEQ_TILE, ROW_BLOCK):
        rows = slice(r0, r0 + ROW_BLOCK)
        u = r.proj[rows, _CG:_CG + CONV_DIM] * r.proj[rows, _XIN:_XIN + CONV_DIM]
        ext = jnp.concatenate([tail, u], axis=0)
        u1 = pltpu.roll(ext, 1, axis=0)[SUBLANES:]
        u2 = pltpu.roll(ext, 2, axis=0)[SUBLANES:]
        y_conv = r.proj[rows, _BG:_BG + CONV_DIM] * (w0 * u2 + w1 * u1 + w2 * u)
        r.ymix[rows, 0:CONV_DIM] = y_conv.astype(BF16)
        tail = u[ROW_BLOCK - SUBLANES:]
    r.ucarry[...] = tail


def _retention(r, tile_in_seq, fresh):
    half = HEAD_DIM // 2
    lane = lax.broadcasted_iota(jnp.int32, (1, HEAD_DIM), 1)
    sign = jnp.where(lane < half, -1.0, 1.0)
    inv_freq = _inv_freq()

    for s in range(SEQ_TILE // RET_TILE):
        rows = slice(s * RET_TILE, (s + 1) * RET_TILE)
        p0 = r.pos[0, pl.ds(tile_in_seq * (SEQ_TILE // LANES) + s * (RET_TILE // LANES), 1), 0:1]
        base = p0.astype(F32) * inv_freq
        cos_b = jnp.cos(base)
        sin_b = jnp.sin(base)
        for r0 in range(0, RET_TILE, ROW_BLOCK):
            blk = slice(r0, r0 + ROW_BLOCK)
            cosr = r.cosr[blk, :]
            sinr = r.sinr[blk, :]
            r.cos2[blk, :] = cosr * cos_b - sinr * sin_b
            r.sin2[blk, :] = sinr * (cos_b * sign) + cosr * (sin_b * sign)

        for hd in range(RET_HEADS):
            c0 = hd * HEAD_DIM
            q = r.proj[rows, _Q + c0:_Q + c0 + HEAD_DIM]
            k = r.proj[rows, _K + c0:_K + c0 + HEAD_DIM]
            v = r.proj[rows, _V + c0:_V + c0 + HEAD_DIM].astype(BF16)
            g = r.proj[rows, _G + c0:_G + c0 + HEAD_DIM]
            qr = q * r.cos2[...] + pltpu.roll(q, half, axis=1) * r.sin2[...]
            kr = k * r.cos2[...] + pltpu.roll(k, half, axis=1) * r.sin2[...]
            scores = lax.dot_general(qr.astype(BF16), kr.astype(BF16), (((1,), (1,)), ((), ())),
                                     preferred_element_type=F32)
            p = (scores * r.dmat[hd]).astype(BF16)
            state = r.state[hd]
            if s == 0:
                state = jnp.where(fresh, 0.0, state)
            o = _dot(p, v) + _dot((qr * r.qdec[hd]).astype(BF16), state.astype(BF16))
            kd_t = (kr * r.kdec[hd]).T.astype(BF16)
            r.state[hd] = state * math.exp(_LOG_GAMMA[hd] * RET_TILE) + _dot(kd_t, v)
            y = (g * jax.nn.sigmoid(g)) * (o * _rms_scale(o))
            r.ymix[rows, CONV_DIM + c0:CONV_DIM + c0 + HEAD_DIM] = y.astype(BF16)


def _mixer_kernel(x, mod, pos, gpre, gpost, win_hbm, convw, wout_hbm, out,
                  win, wout, stage_in, stage_out, sem_in, sem_out,
                  h, proj, ymix, mix, state, ucarry, dmat, qdec, kdec, cosr, sinr, cos2, sin2):
    tile_in_seq = pl.program_id(1)
    r = types.SimpleNamespace(
        x=x, mod=mod, pos=pos, gpre=gpre, gpost=gpost, convw=convw, out=out, h=h, proj=proj, ymix=ymix,
        mix=mix, state=state, ucarry=ucarry, dmat=dmat, qdec=qdec, kdec=kdec, cosr=cosr, sinr=sinr,
        cos2=cos2, sin2=sin2)

    @pl.when(jnp.logical_and(pl.program_id(0) == 0, tile_in_seq == 0))
    def _():
        _load_weight_bf16(win_hbm, win, stage_in, sem_in)
        _load_weight_bf16(wout_hbm, wout, stage_out, sem_out)
        _fill_tables(r)
        state[...] = jnp.zeros_like(state)
        ucarry[...] = jnp.zeros_like(ucarry)

    fresh = tile_in_seq == 0
    _norm(r)
    proj[:, 0:IN_DIM] = _dot(h[:, 0:D_MODEL], win[...])
    _conv(r, fresh)
    _retention(r, tile_in_seq, fresh)
    mix[:, 0:D_MODEL] = _dot(ymix[:, 0:D_MODEL], wout[...])
    _finish(r)


def _mixer(x, mod, pos, g_pre, g_post, w_in, conv_w, w_out):
    batch, seq, _ = x.shape
    fixed = lambda b, t: (0, 0)
    tile = (SEQ_TILE, D_MODEL)
    padded = (SEQ_TILE, D_MODEL + LANES)
    return pl.pallas_call(
        _mixer_kernel,
        grid=(batch, seq // SEQ_TILE),
        in_specs=[
            pl.BlockSpec((1,) + tile, lambda b, t: (b, t, 0)),
            pl.BlockSpec((N_MOD, batch, D_MODEL), lambda b, t: (0, 0, 0)),
            pl.BlockSpec((1, seq // LANES, LANES), lambda b, t: (b, 0, 0)),
            pl.BlockSpec((1, D_MODEL), fixed),
            pl.BlockSpec((1, D_MODEL), fixed),
            pl.BlockSpec(memory_space=pl.ANY),
            pl.BlockSpec((CONV_WIDTH, CONV_DIM), fixed),
            pl.BlockSpec(memory_space=pl.ANY),
        ],
        out_specs=pl.BlockSpec((1,) + tile, lambda b, t: (b, t, 0)),
        out_shape=jax.ShapeDtypeStruct(x.shape, F32),
        scratch_shapes=[
            pltpu.VMEM((D_MODEL, IN_DIM), BF16),
            pltpu.VMEM((D_MODEL, D_MODEL), BF16),
            pltpu.VMEM(_stage_shape(IN_DIM), F32),
            pltpu.VMEM(_stage_shape(D_MODEL), F32),
            pltpu.SemaphoreType.DMA((W_SLOTS,)),
            pltpu.SemaphoreType.DMA((W_SLOTS,)),
            pltpu.VMEM(padded, BF16),
            pltpu.VMEM((SEQ_TILE, IN_DIM + LANES), F32),
            pltpu.VMEM(padded, BF16),
            pltpu.VMEM(padded, F32),
            pltpu.VMEM((RET_HEADS, HEAD_DIM, HEAD_DIM), F32),
            pltpu.VMEM((SUBLANES, CONV_DIM), F32),
            pltpu.VMEM((RET_HEADS, RET_TILE, RET_TILE), F32),
            pltpu.VMEM((RET_HEADS, RET_TILE, HEAD_DIM), F32),
            pltpu.VMEM((RET_HEADS, RET_TILE, HEAD_DIM), F32),
            pltpu.VMEM((RET_TILE, HEAD_DIM), F32),
            pltpu.VMEM((RET_TILE, HEAD_DIM), F32),
            pltpu.VMEM((RET_TILE, HEAD_DIM), F32),
            pltpu.VMEM((RET_TILE, HEAD_DIM), F32),
        ],
        compiler_params=pltpu.CompilerParams(
            dimension_semantics=("arbitrary", "arbitrary"), vmem_limit_bytes=VMEM_LIMIT),
        name="mixer",
    )(x, mod, pos, g_pre, g_post, w_in, conv_w, w_out)


def _mlp_kernel(x_ref, mod_ref, gpre_ref, gpost_ref, w1_hbm, w2_hbm, o_ref,
                w1_ref, w2_ref, stage1, stage2, sem1, sem2, act_ref):
    @pl.when(jnp.logical_and(pl.program_id(0) == 0, pl.program_id(1) == 0))
    def _():
        _load_weight_bf16(w1_hbm, w1_ref, stage1, sem1)
        _load_weight_bf16(w2_hbm, w2_ref, stage2, sem2)

    shift = _mod_row(mod_ref, 3)
    scale = _mod_row(mod_ref, 4)
    gate = _mod_row(mod_ref, 5)
    x = x_ref[0]
    h = ((x * _rms_scale(x)) * (gpre_ref[...] * (1.0 + scale)) + shift).astype(BF16)
    for j in range(D_FF // FF_BLOCK):
        a = jnp.maximum(_dot(h, w1_ref[:, j * FF_BLOCK:(j + 1) * FF_BLOCK]), 0.0)
        act_ref[:, j * FF_BLOCK:(j + 1) * FF_BLOCK] = (a * a).astype(BF16)
    f = _dot(act_ref[:, 0:D_FF], w2_ref[...])
    o_ref[0] = x_ref[0] + (f * _rms_scale(f)) * (gate * gpost_ref[...])


def _mlp(x, mod, g_pre, g_post, w1, w2):
    batch, seq, _ = x.shape
    return pl.pallas_call(
        _mlp_kernel,
        grid=(batch, seq // MLP_TILE),
        in_specs=[
            pl.BlockSpec((1, MLP_TILE, D_MODEL), lambda b, t: (b, t, 0)),
            pl.BlockSpec((N_MOD, batch, D_MODEL), lambda b, t: (0, 0, 0)),
            pl.BlockSpec((1, D_MODEL), lambda b, t: (0, 0)),
            pl.BlockSpec((1, D_MODEL), lambda b, t: (0, 0)),
            pl.BlockSpec(memory_space=pl.ANY),
            pl.BlockSpec(memory_space=pl.ANY),
        ],
        out_specs=pl.BlockSpec((1, MLP_TILE, D_MODEL), lambda b, t: (b, t, 0)),
        out_shape=jax.ShapeDtypeStruct(x.shape, F32),
        scratch_shapes=[
            pltpu.VMEM((D_MODEL, D_FF), BF16),
            pltpu.VMEM((D_FF, D_MODEL), BF16),
            pltpu.VMEM(_stage_shape(D_FF), F32),
            pltpu.VMEM(_stage_shape(D_MODEL), F32),
            pltpu.SemaphoreType.DMA((W_SLOTS,)),
            pltpu.SemaphoreType.DMA((W_SLOTS,)),
            pltpu.VMEM((MLP_TILE, D_FF + LANES), BF16),
        ],
        compiler_params=pltpu.CompilerParams(
            dimension_semantics=("arbitrary", "arbitrary"), vmem_limit_bytes=VMEM_LIMIT),
        name="mlp",
    )(x, mod, g_pre, g_post, w1, w2)


def kernel(x, c, positions, w_ada, b_ada, g_pre_mix, g_post_mix, w_in, conv_w, w_out,
           g_pre_mlp, g_post_mlp, w_fc1, w_fc2):
    batch, seq, d = x.shape
    assert d == D_MODEL and seq % SEQ_TILE == 0 and seq % MLP_TILE == 0
    assert SEQ_TILE % RET_TILE == 0 and RET_TILE % CHUNK == 0 and RET_TILE % LANES == 0
    pos = positions.reshape(batch, seq // LANES, LANES)
    for layer in range(w_ada.shape[0]):
        mod = _ada(c, w_ada[layer], b_ada[layer])
        x = _mixer(x, mod, pos, g_pre_mix[layer][None], g_post_mix[layer][None],
                   w_in[layer], conv_w[layer], w_out[layer])
        x = _mlp(x, mod, g_pre_mlp[layer][None], g_post_mlp[layer][None], w_fc1[layer], w_fc2[layer])
    return x
```

```python
import math
import types

import jax
import jax.numpy as jnp
from jax import lax
from jax.experimental import pallas as pl
from jax.experimental.pallas import tpu as pltpu

F32 = jnp.float32
BF16 = jnp.bfloat16

D_MODEL = 1024
CHUNK = 64
CONV_WIDTH = 3
CONV_DIM = D_MODEL // 2
RET_HEADS = 4
RET_DIM = D_MODEL - CONV_DIM
HEAD_DIM = RET_DIM // RET_HEADS
IN_DIM = 3 * CONV_DIM + 4 * RET_DIM
D_FF = 4 * D_MODEL
ROPE_BASE = 10000.0
DECAY_OFFSET = 5.0
EPS = 1e-6
N_MOD = 6

LANES = 128
SUBLANES = 8
SEQ_TILE = 1024
RET_TILE = 256
ROW_BLOCK = 32
MLP_TILE = 1024
FF_BLOCK = 512
ADA_BLOCK = 2048
W_SLOTS = 4
W_CHUNK_BYTES = 1 << 20
VMEM_LIMIT = 58 * 1024 * 1024

_XIN, _BG, _CG = 0, CONV_DIM, 2 * CONV_DIM
_Q = 3 * CONV_DIM
_K = _Q + RET_DIM
_V = _K + RET_DIM
_G = _V + RET_DIM

_LOG_GAMMA = [math.log1p(-2.0 ** (-DECAY_OFFSET - h)) for h in range(RET_HEADS)]


def _stage_shape(cols):
    rows = 1 << int(math.log2(W_CHUNK_BYTES // (4 * cols)))
    return (W_SLOTS, rows, cols)


def _dot(a, b):
    return jnp.dot(a, b, preferred_element_type=F32)


def _rms_scale(v):
    return lax.rsqrt(jnp.mean(v * v, axis=-1, keepdims=True) + EPS)


def _load_weight_bf16(w_hbm, w_vmem, stage, sem):
    slots, chunk = stage.shape[0], stage.shape[1]
    n_chunks = w_hbm.shape[0] // chunk

    def copy(k):
        return pltpu.make_async_copy(
            w_hbm.at[pl.ds(k * chunk, chunk), :], stage.at[k % slots], sem.at[k % slots])

    for k in range(min(slots - 1, n_chunks)):
        copy(k).start()
    for k in range(n_chunks):
        if k + slots - 1 < n_chunks:
            copy(k + slots - 1).start()
        copy(k).wait()
        w_vmem[k * chunk:(k + 1) * chunk, :] = stage[k % slots].astype(BF16)


def _ada_kernel(c_ref, w_ref, b_ref, o_ref):
    c = c_ref[...]
    a = (c * jax.nn.sigmoid(c)).astype(BF16)
    mod = _dot(a, w_ref[...].astype(BF16)) + b_ref[...]
    for m in range(ADA_BLOCK // D_MODEL):
        o_ref[m] = mod[:, m * D_MODEL:(m + 1) * D_MODEL]


def _ada(c, w_ada, b_ada):
    batch = c.shape[0]
    n = w_ada.shape[1]
    per_step = ADA_BLOCK // D_MODEL
    return pl.pallas_call(
        _ada_kernel,
        grid=(n // ADA_BLOCK,),
        in_specs=[
            pl.BlockSpec((batch, D_MODEL), lambda j: (0, 0)),
            pl.BlockSpec((D_MODEL, ADA_BLOCK), lambda j: (0, j)),
            pl.BlockSpec((1, ADA_BLOCK), lambda j: (0, j)),
        ],
        out_specs=pl.BlockSpec((per_step, batch, D_MODEL), lambda j: (j, 0, 0)),
        out_shape=jax.ShapeDtypeStruct((n // D_MODEL, batch, D_MODEL), F32),
        compiler_params=pltpu.CompilerParams(dimension_semantics=("arbitrary",)),
        name="ada_mod",
    )(c, w_ada, b_ada)


def _mod_row(mod_ref, k):
    return mod_ref[k, pl.ds(pl.program_id(0), 1), :]


def _inv_freq():
    half = HEAD_DIM // 2
    lane = lax.broadcasted_iota(jnp.int32, (1, HEAD_DIM), 1)
    return jnp.exp((lane % half).astype(F32) * (-math.log(ROPE_BASE) / half))


def _fill_tables(r):
    k_scale = HEAD_DIM ** -0.5
    i = lax.broadcasted_iota(jnp.int32, (RET_TILE, RET_TILE), 0)
    j = lax.broadcasted_iota(jnp.int32, (RET_TILE, RET_TILE), 1)
    ci = i // CHUNK
    cj = j // CHUNK
    dist = jnp.where(ci == cj, jnp.abs(i - j), i - j).astype(F32)
    t = lax.broadcasted_iota(jnp.int32, (RET_TILE, HEAD_DIM), 0).astype(F32)
    for h in range(RET_HEADS):
        lg = _LOG_GAMMA[h]
        r.dmat[h] = jnp.where(cj <= ci, k_scale * jnp.exp(lg * dist), 0.0)
        r.qdec[h] = jnp.exp(lg * (t + 1.0))
        r.kdec[h] = k_scale * jnp.exp(lg * (RET_TILE - 1.0 - t))
    ang = t * _inv_freq()
    r.cosr[...] = jnp.cos(ang)
    r.sinr[...] = jnp.sin(ang)


def _norm(r):
    gain = r.gpre[...] * (1.0 + _mod_row(r.mod, 1))
    shift = _mod_row(r.mod, 0)
    for r0 in range(0, SEQ_TILE, ROW_BLOCK):
        rows = slice(r0, r0 + ROW_BLOCK)
        x = r.x[0, rows, :]
        r.h[rows, 0:D_MODEL] = ((x * _rms_scale(x)) * gain + shift).astype(BF16)


def _finish(r):
    gain = _mod_row(r.mod, 2) * r.gpost[...]
    for r0 in range(0, SEQ_TILE, ROW_BLOCK):
        rows = slice(r0, r0 + ROW_BLOCK)
        mix = r.mix[rows, 0:D_MODEL]
        r.out[0, rows, :] = r.x[0, rows, :] + (mix * _rms_scale(mix)) * gain


def _conv(r, fresh):
    pad = SUBLANES
    r.ubuf[0:pad, :] = jnp.where(fresh, 0.0, r.ubuf[SEQ_TILE:SEQ_TILE + pad, :])
    w0, w1, w2 = r.convw[0:1, :], r.convw[1:2, :], r.convw[2:3, :]
    for r0 in range(0, SEQ_TILE, ROW_BLOCK):
        rows = slice(r0, r0 + ROW_BLOCK)
        u = r.proj[rows, _CG:_CG + CONV_DIM] * r.proj[rows, _XIN:_XIN + CONV_DIM]
        r.ubuf[pad + r0:pad + r0 + ROW_BLOCK, :] = u
        u1 = r.ubuf[pad + r0 - 1:pad + r0 - 1 + ROW_BLOCK, :]
        u2 = r.ubuf[pad + r0 - 2:pad + r0 - 2 + ROW_BLOCK, :]
        y_conv = r.proj[rows, _BG:_BG + CONV_DIM] * (w0 * u2 + w1 * u1 + w2 * u)
        r.ymix[rows, 0:CONV_DIM] = y_conv.astype(BF16)


def _retention(r, tile_in_seq, fresh):
    half = HEAD_DIM // 2
    lane = lax.broadcasted_iota(jnp.int32, (1, HEAD_DIM), 1)
    sign = jnp.where(lane < half, -1.0, 1.0)
    inv_freq = _inv_freq()

    for s in range(SEQ_TILE // RET_TILE):
        rows = slice(s * RET_TILE, (s + 1) * RET_TILE)
        p0 = r.pos[0, pl.ds(tile_in_seq * (SEQ_TILE // LANES) + s * (RET_TILE // LANES), 1), 0:1]
        base = p0.astype(F32) * inv_freq
        cos_b = jnp.cos(base)
        sin_b = jnp.sin(base)
        for r0 in range(0, RET_TILE, ROW_BLOCK):
            blk = slice(r0, r0 + ROW_BLOCK)
            cosr = r.cosr[blk, :]
            sinr = r.sinr[blk, :]
            r.cos2[blk, :] = cosr * cos_b - sinr * sin_b
            r.sin2[blk, :] = sinr * (cos_b * sign) + cosr * (sin_b * sign)

        for hd in range(RET_HEADS):
            c0 = hd * HEAD_DIM
            q = r.proj[rows, _Q + c0:_Q + c0 + HEAD_DIM]
            k = r.proj[rows, _K + c0:_K + c0 + HEAD_DIM]
            v = r.proj[rows, _V + c0:_V + c0 + HEAD_DIM].astype(BF16)
            g = r.proj[rows, _G + c0:_G + c0 + HEAD_DIM]
            qr = q * r.cos2[...] + pltpu.roll(q, half, axis=1) * r.sin2[...]
            kr = k * r.cos2[...] + pltpu.roll(k, half, axis=1) * r.sin2[...]
            scores = lax.dot_general(qr.astype(BF16), kr.astype(BF16), (((1,), (1,)), ((), ())),
                                     preferred_element_type=F32)
            p = (scores * r.dmat[hd]).astype(BF16)
            state = r.state[hd]
            if s == 0:
                state = jnp.where(fresh, 0.0, state)
            o = _dot(p, v) + _dot((qr * r.qdec[hd]).astype(BF16), state.astype(BF16))
            kd_t = (kr * r.kdec[hd]).T.astype(BF16)
            r.state[hd] = state * math.exp(_LOG_GAMMA[hd] * RET_TILE) + _dot(kd_t, v)
            y = (g * jax.nn.sigmoid(g)) * (o * _rms_scale(o))
            r.ymix[rows, CONV_DIM + c0:CONV_DIM + c0 + HEAD_DIM] = y.astype(BF16)


def _mixer_kernel(x, mod, pos, gpre, gpost, win_hbm, convw, wout_hbm, out,
                  win, wout, stage_in, stage_out, sem_in, sem_out,
                  h, proj, ymix, mix, state, ubuf, dmat, qdec, kdec, cosr, sinr, cos2, sin2):
    tile_in_seq = pl.program_id(1)
    r = types.SimpleNamespace(
        x=x, mod=mod, pos=pos, gpre=gpre, gpost=gpost, convw=convw, out=out, h=h, proj=proj, ymix=ymix,
        mix=mix, state=state, ubuf=ubuf, dmat=dmat, qdec=qdec, kdec=kdec, cosr=cosr, sinr=sinr,
        cos2=cos2, sin2=sin2)

    @pl.when(jnp.logical_and(pl.program_id(0) == 0, tile_in_seq == 0))
    def _():
        _load_weight_bf16(win_hbm, win, stage_in, sem_in)
        _load_weight_bf16(wout_hbm, wout, stage_out, sem_out)
        _fill_tables(r)
        state[...] = jnp.zeros_like(state)
        ubuf[SEQ_TILE:SEQ_TILE + SUBLANES, :] = jnp.zeros((SUBLANES, CONV_DIM), F32)

    fresh = tile_in_seq == 0
    _norm(r)
    proj[:, 0:IN_DIM] = _dot(h[:, 0:D_MODEL], win[...])
    _conv(r, fresh)
    _retention(r, tile_in_seq, fresh)
    mix[:, 0:D_MODEL] = _dot(ymix[:, 0:D_MODEL], wout[...])
    _finish(r)


def _mixer(x, mod, pos, g_pre, g_post, w_in, conv_w, w_out):
    batch, seq, _ = x.shape
    fixed = lambda b, t: (0, 0)
    tile = (SEQ_TILE, D_MODEL)
    padded = (SEQ_TILE, D_MODEL + LANES)
    return pl.pallas_call(
        _mixer_kernel,
        grid=(batch, seq // SEQ_TILE),
        in_specs=[
            pl.BlockSpec((1,) + tile, lambda b, t: (b, t, 0)),
            pl.BlockSpec((N_MOD, batch, D_MODEL), lambda b, t: (0, 0, 0)),
            pl.BlockSpec((1, seq // LANES, LANES), lambda b, t: (b, 0, 0)),
            pl.BlockSpec((1, D_MODEL), fixed),
            pl.BlockSpec((1, D_MODEL), fixed),
            pl.BlockSpec(memory_space=pl.ANY),
            pl.BlockSpec((CONV_WIDTH, CONV_DIM), fixed),
            pl.BlockSpec(memory_space=pl.ANY),
        ],
        out_specs=pl.BlockSpec((1,) + tile, lambda b, t: (b, t, 0)),
        out_shape=jax.ShapeDtypeStruct(x.shape, F32),
        scratch_shapes=[
            pltpu.VMEM((D_MODEL, IN_DIM), BF16),
            pltpu.VMEM((D_MODEL, D_MODEL), BF16),
            pltpu.VMEM(_stage_shape(IN_DIM), F32),
            pltpu.VMEM(_stage_shape(D_MODEL), F32),
            pltpu.SemaphoreType.DMA((W_SLOTS,)),
            pltpu.SemaphoreType.DMA((W_SLOTS,)),
            pltpu.VMEM(padded, BF16),
            pltpu.VMEM((SEQ_TILE, IN_DIM + LANES), F32),
            pltpu.VMEM(padded, BF16),
            pltpu.VMEM(padded, F32),
            pltpu.VMEM((RET_HEADS, HEAD_DIM, HEAD_DIM), F32),
            pltpu.VMEM((SEQ_TILE + SUBLANES, CONV_DIM), F32),
            pltpu.VMEM((RET_HEADS, RET_TILE, RET_TILE), F32),
            pltpu.VMEM((RET_HEADS, RET_TILE, HEAD_DIM), F32),
            pltpu.VMEM((RET_HEADS, RET_TILE, HEAD_DIM), F32),
            pltpu.VMEM((RET_TILE, HEAD_DIM), F32),
            pltpu.VMEM((RET_TILE, HEAD_DIM), F32),
            pltpu.VMEM((RET_TILE, HEAD_DIM), F32),
            pltpu.VMEM((RET_TILE, HEAD_DIM), F32),
        ],
        compiler_params=pltpu.CompilerParams(
            dimension_semantics=("arbitrary", "arbitrary"), vmem_limit_bytes=VMEM_LIMIT),
        name="mixer",
    )(x, mod, pos, g_pre, g_post, w_in, conv_w, w_out)


def _mlp_kernel(x_ref, mod_ref, gpre_ref, gpost_ref, w1_hbm, w2_hbm, o_ref,
                w1_ref, w2_ref, stage1, stage2, sem1, sem2, act_ref):
    @pl.when(jnp.logical_and(pl.program_id(0) == 0, pl.program_id(1) == 0))
    def _():
        _load_weight_bf16(w1_hbm, w1_ref, stage1, sem1)
        _load_weight_bf16(w2_hbm, w2_ref, stage2, sem2)

    shift = _mod_row(mod_ref, 3)
    scale = _mod_row(mod_ref, 4)
    gate = _mod_row(mod_ref, 5)
    x = x_ref[0]
    h = ((x * _rms_scale(x)) * (gpre_ref[...] * (1.0 + scale)) + shift).astype(BF16)
    for j in range(D_FF // FF_BLOCK):
        a = jnp.maximum(_dot(h, w1_ref[:, j * FF_BLOCK:(j + 1) * FF_BLOCK]), 0.0)
        act_ref[:, j * FF_BLOCK:(j + 1) * FF_BLOCK] = (a * a).astype(BF16)
    f = _dot(act_ref[:, 0:D_FF], w2_ref[...])
    o_ref[0] = x_ref[0] + (f * _rms_scale(f)) * (gate * gpost_ref[...])


def _mlp(x, mod, g_pre, g_post, w1, w2):
    batch, seq, _ = x.shape
    return pl.pallas_call(
        _mlp_kernel,
        grid=(batch, seq // MLP_TILE),
        in_specs=[
            pl.BlockSpec((1, MLP_TILE, D_MODEL), lambda b, t: (b, t, 0)),
            pl.BlockSpec((N_MOD, batch, D_MODEL), lambda b, t: (0, 0, 0)),
            pl.BlockSpec((1, D_MODEL), lambda b, t: (0, 0)),
            pl.BlockSpec((1, D_MODEL), lambda b, t: (0, 0)),
            pl.BlockSpec(memory_space=pl.ANY),
            pl.BlockSpec(memory_space=pl.ANY),
        ],
        out_specs=pl.BlockSpec((1, MLP_TILE, D_MODEL), lambda b, t: (b, t, 0)),
        out_shape=jax.ShapeDtypeStruct(x.shape, F32),
        scratch_shapes=[
            pltpu.VMEM((D_MODEL, D_FF), BF16),
            pltpu.VMEM((D_FF, D_MODEL), BF16),
            pltpu.VMEM(_stage_shape(D_FF), F32),
            pltpu.VMEM(_stage_shape(D_MODEL), F32),
            pltpu.SemaphoreType.DMA((W_SLOTS,)),
            pltpu.SemaphoreType.DMA((W_SLOTS,)),
            pltpu.VMEM((MLP_TILE, D_FF + LANES), BF16),
        ],
        compiler_params=pltpu.CompilerParams(
            dimension_semantics=("arbitrary", "arbitrary"), vmem_limit_bytes=VMEM_LIMIT),
        name="mlp",
    )(x, mod, g_pre, g_post, w1, w2)


def kernel(x, c, positions, w_ada, b_ada, g_pre_mix, g_post_mix, w_in, conv_w, w_out,
           g_pre_mlp, g_post_mlp, w_fc1, w_fc2):
    batch, seq, d = x.shape
    assert d == D_MODEL and seq % SEQ_TILE == 0 and seq % MLP_TILE == 0
    assert SEQ_TILE % RET_TILE == 0 and RET_TILE % CHUNK == 0 and RET_TILE % LANES == 0
    pos = positions.reshape(batch, seq // LANES, LANES)
    for layer in range(w_ada.shape[0]):
        one = slice(layer, layer + 1)
        mod = _ada(c, w_ada[layer], b_ada[one])
        x = _mixer(x, mod, pos, g_pre_mix[one], g_post_mix[one], w_in[layer], conv_w[layer], w_out[layer])
        x = _mlp(x, mod, g_pre_mlp[one], g_post_mlp[one], w_fc1[layer], w_fc2[layer])
    return x
```

```python
import math
import types

import jax
import jax.numpy as jnp
from jax import lax
from jax.experimental import pallas as pl
from jax.experimental.pallas import tpu as pltpu

F32 = jnp.float32
BF16 = jnp.bfloat16

D_MODEL = 1024
CHUNK = 64
CONV_WIDTH = 3
CONV_DIM = D_MODEL // 2
RET_HEADS = 4
RET_DIM = D_MODEL - CONV_DIM
HEAD_DIM = RET_DIM // RET_HEADS
IN_DIM = 3 * CONV_DIM + 4 * RET_DIM
D_FF = 4 * D_MODEL
ROPE_BASE = 10000.0
DECAY_OFFSET = 5.0
EPS = 1e-6
N_MOD = 6

LANES = 128
SUBLANES = 8
SEQ_TILE = 1024
RET_TILE = 256
ROW_BLOCK = 32
MLP_TILE = 1024
FF_BLOCK = 512
ADA_BLOCK = 2048
W_SLOTS = 4
W_CHUNK_BYTES = 1 << 20
VMEM_LIMIT = 58 * 1024 * 1024

_XIN, _BG, _CG = 0, CONV_DIM, 2 * CONV_DIM
_Q = 3 * CONV_DIM
_K = _Q + RET_DIM
_V = _K + RET_DIM
_G = _V + RET_DIM

_LOG_GAMMA = [math.log1p(-2.0 ** (-DECAY_OFFSET - h)) for h in range(RET_HEADS)]


def _stage_shape(cols):
    rows = 1 << int(math.log2(W_CHUNK_BYTES // (4 * cols)))
    return (W_SLOTS, rows, cols)


def _dot(a, b):
    return jnp.dot(a, b, preferred_element_type=F32)


def _rms_scale(v):
    return lax.rsqrt(jnp.mean(v * v, axis=-1, keepdims=True) + EPS)


def _load_weight_bf16(w_hbm, w_vmem, stage, sem):
    slots, chunk = stage.shape[0], stage.shape[1]
    n_chunks = w_hbm.shape[0] // chunk

    def copy(k):
        return pltpu.make_async_copy(
            w_hbm.at[pl.ds(k * chunk, chunk), :], stage.at[k % slots], sem.at[k % slots])

    for k in range(min(slots - 1, n_chunks)):
        copy(k).start()
    for k in range(n_chunks):
        if k + slots - 1 < n_chunks:
            copy(k + slots - 1).start()
        copy(k).wait()
        w_vmem[k * chunk:(k + 1) * chunk, :] = stage[k % slots].astype(BF16)


def _ada_kernel(c_ref, w_ref, b_ref, o_ref):
    c = c_ref[...]
    a = (c * jax.nn.sigmoid(c)).astype(BF16)
    mod = _dot(a, w_ref[...].astype(BF16)) + b_ref[...]
    for m in range(ADA_BLOCK // D_MODEL):
        o_ref[m] = mod[:, m * D_MODEL:(m + 1) * D_MODEL]


def _ada(c, w_ada, b_ada):
    batch = c.shape[0]
    n = w_ada.shape[1]
    per_step = ADA_BLOCK // D_MODEL
    return pl.pallas_call(
        _ada_kernel,
        grid=(n // ADA_BLOCK,),
        in_specs=[
            pl.BlockSpec((batch, D_MODEL), lambda j: (0, 0)),
            pl.BlockSpec((D_MODEL, ADA_BLOCK), lambda j: (0, j)),
            pl.BlockSpec((1, ADA_BLOCK), lambda j: (0, j)),
        ],
        out_specs=pl.BlockSpec((per_step, batch, D_MODEL), lambda j: (j, 0, 0)),
        out_shape=jax.ShapeDtypeStruct((n // D_MODEL, batch, D_MODEL), F32),
        compiler_params=pltpu.CompilerParams(dimension_semantics=("arbitrary",)),
        name="ada_mod",
    )(c, w_ada, b_ada)


def _mod_row(mod_ref, k):
    return mod_ref[k, pl.ds(pl.program_id(0), 1), :]


def _inv_freq():
    half = HEAD_DIM // 2
    lane = lax.broadcasted_iota(jnp.int32, (1, HEAD_DIM), 1)
    return jnp.exp((lane % half).astype(F32) * (-math.log(ROPE_BASE) / half))


def _fill_tables(r):
    k_scale = HEAD_DIM ** -0.5
    i = lax.broadcasted_iota(jnp.int32, (RET_TILE, RET_TILE), 0)
    j = lax.broadcasted_iota(jnp.int32, (RET_TILE, RET_TILE), 1)
    ci = i // CHUNK
    cj = j // CHUNK
    dist = jnp.where(ci == cj, jnp.abs(i - j), i - j).astype(F32)
    t = lax.broadcasted_iota(jnp.int32, (RET_TILE, HEAD_DIM), 0).astype(F32)
    for h in range(RET_HEADS):
        lg = _LOG_GAMMA[h]
        r.dmat[h] = jnp.where(cj <= ci, k_scale * jnp.exp(lg * dist), 0.0)
        r.qdec[h] = jnp.exp(lg * (t + 1.0))
        r.kdec[h] = k_scale * jnp.exp(lg * (RET_TILE - 1.0 - t))
    ang = t * _inv_freq()
    r.cosr[...] = jnp.cos(ang)
    r.sinr[...] = jnp.sin(ang)


def _norm(r):
    gain = r.gpre[...] * (1.0 + _mod_row(r.mod, 1))
    shift = _mod_row(r.mod, 0)
    for r0 in range(0, SEQ_TILE, ROW_BLOCK):
        rows = slice(r0, r0 + ROW_BLOCK)
        x = r.x[0, rows, :]
        r.h[rows, :] = ((x * _rms_scale(x)) * gain + shift).astype(BF16)


def _finish(r):
    gain = _mod_row(r.mod, 2) * r.gpost[...]
    for r0 in range(0, SEQ_TILE, ROW_BLOCK):
        rows = slice(r0, r0 + ROW_BLOCK)
        mix = r.mix[rows, :]
        r.out[0, rows, :] = r.x[0, rows, :] + (mix * _rms_scale(mix)) * gain


def _conv(r, fresh):
    tail = jnp.where(fresh, 0.0, r.ucarry[...])
    w0, w1, w2 = r.convw[0:1, :], r.convw[1:2, :], r.convw[2:3, :]
    for r0 in range(0, SEQ_TILE, ROW_BLOCK):
        rows = slice(r0, r0 + ROW_BLOCK)
        u = r.proj[rows, _CG:_CG + CONV_DIM] * r.proj[rows, _XIN:_XIN + CONV_DIM]
        ext = jnp.concatenate([tail, u], axis=0)
        u1 = pltpu.roll(ext, 1, axis=0)[SUBLANES:]
        u2 = pltpu.roll(ext, 2, axis=0)[SUBLANES:]
        y_conv = r.proj[rows, _BG:_BG + CONV_DIM] * (w0 * u2 + w1 * u1 + w2 * u)
        r.ymix[rows, 0:CONV_DIM] = y_conv.astype(BF16)
        tail = u[ROW_BLOCK - SUBLANES:]
    r.ucarry[...] = tail


def _retention(r, tile_in_seq, fresh):
    half = HEAD_DIM // 2
    lane = lax.broadcasted_iota(jnp.int32, (1, HEAD_DIM), 1)
    sign = jnp.where(lane < half, -1.0, 1.0)
    inv_freq = _inv_freq()

    for s in range(SEQ_TILE // RET_TILE):
        rows = slice(s * RET_TILE, (s + 1) * RET_TILE)
        p0 = r.pos[0, pl.ds(tile_in_seq * (SEQ_TILE // LANES) + s * (RET_TILE // LANES), 1), 0:1]
        base = p0.astype(F32) * inv_freq
        cos_b = jnp.cos(base)
        sin_b = jnp.sin(base)
        for r0 in range(0, RET_TILE, ROW_BLOCK):
            blk = slice(r0, r0 + ROW_BLOCK)
            cosr = r.cosr[blk, :]
            sinr = r.sinr[blk, :]
            r.cos2[blk, :] = cosr * cos_b - sinr * sin_b
            r.sin2[blk, :] = sinr * (cos_b * sign) + cosr * (sin_b * sign)

        for hd in range(RET_HEADS):
            c0 = hd * HEAD_DIM
            q = r.proj[rows, _Q + c0:_Q + c0 + HEAD_DIM]
            k = r.proj[rows, _K + c0:_K + c0 + HEAD_DIM]
            v = r.proj[rows, _V + c0:_V + c0 + HEAD_DIM].astype(BF16)
            g = r.proj[rows, _G + c0:_G + c0 + HEAD_DIM]
            qr = q * r.cos2[...] + pltpu.roll(q, half, axis=1) * r.sin2[...]
            kr = k * r.cos2[...] + pltpu.roll(k, half, axis=1) * r.sin2[...]
            scores = lax.dot_general(qr.astype(BF16), kr.astype(BF16), (((1,), (1,)), ((), ())),
                                     preferred_element_type=F32)
            p = (scores * r.dmat[hd]).astype(BF16)
            state = r.state[hd]
            if s == 0:
                state = jnp.where(fresh, 0.0, state)
            o = _dot(p, v) + _dot((qr * r.qdec[hd]).astype(BF16), state.astype(BF16))
            kd_t = (kr * r.kdec[hd]).T.astype(BF16)
            r.state[hd] = state * math.exp(_LOG_GAMMA[hd] * RET_TILE) + _dot(kd_t, v)
            y = (g * jax.nn.sigmoid(g)) * (o * _rms_scale(o))
            r.ymix[rows, CONV_DIM + c0:CONV_DIM + c0 + HEAD_DIM] = y.astype(BF16)


def _mixer_kernel(x, mod, pos, gpre, gpost, win_hbm, convw, wout_hbm, out,
                  win, wout, stage_in, stage_out, sem_in, sem_out,
                  h, proj, ymix, mix, state, ucarry, dmat, qdec, kdec, cosr, sinr, cos2, sin2):
    tile_in_seq = pl.program_id(1)
    r = types.SimpleNamespace(
        x=x, mod=mod, pos=pos, gpre=gpre, gpost=gpost, convw=convw, out=out, h=h, proj=proj, ymix=ymix,
        mix=mix, state=state, ucarry=ucarry, dmat=dmat, qdec=qdec, kdec=kdec, cosr=cosr, sinr=sinr,
        cos2=cos2, sin2=sin2)

    @pl.when(jnp.logical_and(pl.program_id(0) == 0, tile_in_seq == 0))
    def _():
        _load_weight_bf16(win_hbm, win, stage_in, sem_in)
        _load_weight_bf16(wout_hbm, wout, stage_out, sem_out)
        _fill_tables(r)
        state[...] = jnp.zeros_like(state)
        ucarry[...] = jnp.zeros_like(ucarry)

    fresh = tile_in_seq == 0
    _norm(r)
    proj[...] = _dot(h[...], win[...])
    _conv(r, fresh)
    _retention(r, tile_in_seq, fresh)
    mix[...] = _dot(ymix[...], wout[...])
    _finish(r)


def _mixer(x, mod, pos, g_pre, g_post, w_in, conv_w, w_out):
    batch, seq, _ = x.shape
    fixed = lambda b, t: (0, 0)
    tile = (SEQ_TILE, D_MODEL)
    return pl.pallas_call(
        _mixer_kernel,
        grid=(batch, seq // SEQ_TILE),
        in_specs=[
            pl.BlockSpec((1,) + tile, lambda b, t: (b, t, 0)),
            pl.BlockSpec((N_MOD, batch, D_MODEL), lambda b, t: (0, 0, 0)),
            pl.BlockSpec((1, seq // LANES, LANES), lambda b, t: (b, 0, 0)),
            pl.BlockSpec((1, D_MODEL), fixed),
            pl.BlockSpec((1, D_MODEL), fixed),
            pl.BlockSpec(memory_space=pl.ANY),
            pl.BlockSpec((CONV_WIDTH, CONV_DIM), fixed),
            pl.BlockSpec(memory_space=pl.ANY),
        ],
        out_specs=pl.BlockSpec((1,) + tile, lambda b, t: (b, t, 0)),
        out_shape=jax.ShapeDtypeStruct(x.shape, F32),
        scratch_shapes=[
            pltpu.VMEM((D_MODEL, IN_DIM), BF16),
            pltpu.VMEM((D_MODEL, D_MODEL), BF16),
            pltpu.VMEM(_stage_shape(IN_DIM), F32),
            pltpu.VMEM(_stage_shape(D_MODEL), F32),
            pltpu.SemaphoreType.DMA((W_SLOTS,)),
            pltpu.SemaphoreType.DMA((W_SLOTS,)),
            pltpu.VMEM(tile, BF16),
            pltpu.VMEM((SEQ_TILE, IN_DIM), F32),
            pltpu.VMEM(tile, BF16),
            pltpu.VMEM(tile, F32),
            pltpu.VMEM((RET_HEADS, HEAD_DIM, HEAD_DIM), F32),
            pltpu.VMEM((SUBLANES, CONV_DIM), F32),
            pltpu.VMEM((RET_HEADS, RET_TILE, RET_TILE), F32),
            pltpu.VMEM((RET_HEADS, RET_TILE, HEAD_DIM), F32),
            pltpu.VMEM((RET_HEADS, RET_TILE, HEAD_DIM), F32),
            pltpu.VMEM((RET_TILE, HEAD_DIM), F32),
            pltpu.VMEM((RET_TILE, HEAD_DIM), F32),
            pltpu.VMEM((RET_TILE, HEAD_DIM), F32),
            pltpu.VMEM((RET_TILE, HEAD_DIM), F32),
        ],
        compiler_params=pltpu.CompilerParams(
            dimension_semantics=("arbitrary", "arbitrary"), vmem_limit_bytes=VMEM_LIMIT),
        name="mixer",
    )(x, mod, pos, g_pre, g_post, w_in, conv_w, w_out)


def _mlp_kernel(x_ref, mod_ref, gpre_ref, gpost_ref, w1_hbm, w2_hbm, o_ref,
                w1_ref, w2_ref, stage1, stage2, sem1, sem2, act_ref):
    @pl.when(jnp.logical_and(pl.program_id(0) == 0, pl.program_id(1) == 0))
    def _():
        _load_weight_bf16(w1_hbm, w1_ref, stage1, sem1)
        _load_weight_bf16(w2_hbm, w2_ref, stage2, sem2)

    shift = _mod_row(mod_ref, 3)
    scale = _mod_row(mod_ref, 4)
    gate = _mod_row(mod_ref, 5)
    x = x_ref[0]
    h = ((x * _rms_scale(x)) * (gpre_ref[...] * (1.0 + scale)) + shift).astype(BF16)
    for j in range(D_FF // FF_BLOCK):
        a = jnp.maximum(_dot(h, w1_ref[:, j * FF_BLOCK:(j + 1) * FF_BLOCK]), 0.0)
        act_ref[:, j * FF_BLOCK:(j + 1) * FF_BLOCK] = (a * a).astype(BF16)
    f = _dot(act_ref[...], w2_ref[...])
    o_ref[0] = x_ref[0] + (f * _rms_scale(f)) * (gate * gpost_ref[...])


def _mlp(x, mod, g_pre, g_post, w1, w2):
    batch, seq, _ = x.shape
    return pl.pallas_call(
        _mlp_kernel,
        grid=(batch, seq // MLP_TILE),
        in_specs=[
            pl.BlockSpec((1, MLP_TILE, D_MODEL), lambda b, t: (b, t, 0)),
            pl.BlockSpec((N_MOD, batch, D_MODEL), lambda b, t: (0, 0, 0)),
            pl.BlockSpec((1, D_MODEL), lambda b, t: (0, 0)),
            pl.BlockSpec((1, D_MODEL), lambda b, t: (0, 0)),
            pl.BlockSpec(memory_space=pl.ANY),
            pl.BlockSpec(memory_space=pl.ANY),
        ],
        out_specs=pl.BlockSpec((1, MLP_TILE, D_MODEL), lambda b, t: (b, t, 0)),
        out_shape=jax.ShapeDtypeStruct(x.shape, F32),
        scratch_shapes=[
            pltpu.VMEM((D_MODEL, D_FF), BF16),
            pltpu.VMEM((D_FF, D_MODEL), BF16),
            pltpu.VMEM(_stage_shape(D_FF), F32),
            pltpu.VMEM(_stage_shape(D_MODEL), F32),
            pltpu.SemaphoreType.DMA((W_SLOTS,)),
            pltpu.SemaphoreType.DMA((W_SLOTS,)),
            pltpu.VMEM((MLP_TILE, D_FF), BF16),
        ],
        compiler_params=pltpu.CompilerParams(
            dimension_semantics=("arbitrary", "arbitrary"), vmem_limit_bytes=VMEM_LIMIT),
        name="mlp",
    )(x, mod, g_pre, g_post, w1, w2)


def kernel(x, c, positions, w_ada, b_ada, g_pre_mix, g_post_mix, w_in, conv_w, w_out,
           g_pre_mlp, g_post_mlp, w_fc1, w_fc2):
    batch, seq, d = x.shape
    assert d == D_MODEL and seq % SEQ_TILE == 0 and seq % MLP_TILE == 0
    assert SEQ_TILE % RET_TILE == 0 and RET_TILE % CHUNK == 0 and RET_TILE % LANES == 0
    pos = positions.reshape(batch, seq // LANES, LANES)
    for layer in range(w_ada.shape[0]):
        one = slice(layer, layer + 1)
        mod = _ada(c, w_ada[layer], b_ada[one])
        x = _mixer(x, mod, pos, g_pre_mix[one], g_post_mix[one], w_in[layer], conv_w[layer], w_out[layer])
        x = _mlp(x, mod, g_pre_mlp[one], g_post_mlp[one], w_fc1[layer], w_fc2[layer])
    return x
```

```python
import math
import types

import jax
import jax.numpy as jnp
from jax import lax
from jax.experimental import pallas as pl
from jax.experimental.pallas import tpu as pltpu

F32 = jnp.float32
BF16 = jnp.bfloat16

D_MODEL = 1024
CHUNK = 64
CONV_WIDTH = 3
CONV_DIM = D_MODEL // 2
RET_HEADS = 4
RET_DIM = D_MODEL - CONV_DIM
HEAD_DIM = RET_DIM // RET_HEADS
IN_DIM = 3 * CONV_DIM + 4 * RET_DIM
D_FF = 4 * D_MODEL
ROPE_BASE = 10000.0
DECAY_OFFSET = 5.0
EPS = 1e-6
N_MOD = 6

LANES = 128
SUBLANES = 8
SEQ_TILE = 1024
RET_TILE = 256
ROW_BLOCK = 32
MLP_TILE = 1024
FF_BLOCK = 512
ADA_BLOCK = 2048
W_SLOTS = 4
W_CHUNK_BYTES = 1 << 20
VMEM_LIMIT = 58 * 1024 * 1024

_XIN, _BG, _CG = 0, CONV_DIM, 2 * CONV_DIM
_Q = 3 * CONV_DIM
_K = _Q + RET_DIM
_V = _K + RET_DIM
_G = _V + RET_DIM

_LOG_GAMMA = [math.log1p(-2.0 ** (-DECAY_OFFSET - h)) for h in range(RET_HEADS)]


def _stage_shape(cols):
    rows = 1 << int(math.log2(W_CHUNK_BYTES // (4 * cols)))
    return (W_SLOTS, rows, cols)


def _dot(a, b):
    return jnp.dot(a, b, preferred_element_type=F32)


def _rms_scale(v):
    return lax.rsqrt(jnp.mean(v * v, axis=-1, keepdims=True) + EPS)


def _load_weight_bf16(w_hbm, w_vmem, stage, sem):
    slots, chunk = stage.shape[0], stage.shape[1]
    n_chunks = w_hbm.shape[0] // chunk

    def copy(k):
        return pltpu.make_async_copy(
            w_hbm.at[pl.ds(k * chunk, chunk), :], stage.at[k % slots], sem.at[k % slots])

    for k in range(min(slots - 1, n_chunks)):
        copy(k).start()
    for k in range(n_chunks):
        if k + slots - 1 < n_chunks:
            copy(k + slots - 1).start()
        copy(k).wait()
        w_vmem[k * chunk:(k + 1) * chunk, :] = stage[k % slots].astype(BF16)


def _ada_kernel(c_ref, w_ref, b_ref, o_ref):
    c = c_ref[...]
    a = (c * jax.nn.sigmoid(c)).astype(BF16)
    mod = _dot(a, w_ref[...].astype(BF16)) + b_ref[...]
    for m in range(ADA_BLOCK // D_MODEL):
        o_ref[m] = mod[:, m * D_MODEL:(m + 1) * D_MODEL]


def _ada(c, w_ada, b_ada):
    batch = c.shape[0]
    n = w_ada.shape[1]
    per_step = ADA_BLOCK // D_MODEL
    return pl.pallas_call(
        _ada_kernel,
        grid=(n // ADA_BLOCK,),
        in_specs=[
            pl.BlockSpec((batch, D_MODEL), lambda j: (0, 0)),
            pl.BlockSpec((D_MODEL, ADA_BLOCK), lambda j: (0, j)),
            pl.BlockSpec((1, ADA_BLOCK), lambda j: (0, j)),
        ],
        out_specs=pl.BlockSpec((per_step, batch, D_MODEL), lambda j: (j, 0, 0)),
        out_shape=jax.ShapeDtypeStruct((n // D_MODEL, batch, D_MODEL), F32),
        compiler_params=pltpu.CompilerParams(dimension_semantics=("arbitrary",)),
        name="ada_mod",
    )(c, w_ada, b_ada)


def _mod_row(mod_ref, k, b=None):
    return mod_ref[k, pl.ds(pl.program_id(0) if b is None else b, 1), :]


def _inv_freq():
    half = HEAD_DIM // 2
    lane = lax.broadcasted_iota(jnp.int32, (1, HEAD_DIM), 1)
    return jnp.exp((lane % half).astype(F32) * (-math.log(ROPE_BASE) / half))


def _fill_tables(r):
    k_scale = HEAD_DIM ** -0.5
    i = lax.broadcasted_iota(jnp.int32, (RET_TILE, RET_TILE), 0)
    j = lax.broadcasted_iota(jnp.int32, (RET_TILE, RET_TILE), 1)
    ci = i // CHUNK
    cj = j // CHUNK
    dist = jnp.where(ci == cj, jnp.abs(i - j), i - j).astype(F32)
    t = lax.broadcasted_iota(jnp.int32, (RET_TILE, HEAD_DIM), 0).astype(F32)
    for h in range(RET_HEADS):
        lg = _LOG_GAMMA[h]
        r.dmat[h] = jnp.where(cj <= ci, k_scale * jnp.exp(lg * dist), 0.0)
        r.qdec[h] = jnp.exp(lg * (t + 1.0))
        r.kdec[h] = k_scale * jnp.exp(lg * (RET_TILE - 1.0 - t))
    ang = t * _inv_freq()
    r.cosr[...] = jnp.cos(ang)
    r.sinr[...] = jnp.sin(ang)


def _norm(r):
    gain = r.gpre[...] * (1.0 + _mod_row(r.mod, 1, r.batch))
    shift = _mod_row(r.mod, 0, r.batch)
    for r0 in range(0, SEQ_TILE, ROW_BLOCK):
        rows = slice(r0, r0 + ROW_BLOCK)
        x = r.x[0, rows, :]
        r.h[rows, :] = ((x * _rms_scale(x)) * gain + shift).astype(BF16)


def _finish(r):
    gain = _mod_row(r.mod, 2, r.batch) * r.gpost[...]
    for r0 in range(0, SEQ_TILE, ROW_BLOCK):
        rows = slice(r0, r0 + ROW_BLOCK)
        mix = r.mix[rows, :]
        r.out[0, rows, :] = r.x[0, rows, :] + (mix * _rms_scale(mix)) * gain


def _conv(r, fresh):
    tail = jnp.where(fresh, 0.0, r.ucarry[...])
    w0, w1, w2 = r.convw[0:1, :], r.convw[1:2, :], r.convw[2:3, :]
    for r0 in range(0, SEQ_TILE, ROW_BLOCK):
        rows = slice(r0, r0 + ROW_BLOCK)
        u = r.proj[rows, _CG:_CG + CONV_DIM] * r.proj[rows, _XIN:_XIN + CONV_DIM]
        ext = jnp.concatenate([tail, u], axis=0)
        u1 = pltpu.roll(ext, 1, axis=0)[SUBLANES:]
        u2 = pltpu.roll(ext, 2, axis=0)[SUBLANES:]
        y_conv = r.proj[rows, _BG:_BG + CONV_DIM] * (w0 * u2 + w1 * u1 + w2 * u)
        r.ymix[rows, 0:CONV_DIM] = y_conv.astype(BF16)
        tail = u[ROW_BLOCK - SUBLANES:]
    r.ucarry[...] = tail


def _retention(r, tile_in_seq, fresh):
    half = HEAD_DIM // 2
    lane = lax.broadcasted_iota(jnp.int32, (1, HEAD_DIM), 1)
    sign = jnp.where(lane < half, -1.0, 1.0)
    inv_freq = _inv_freq()

    for s in range(SEQ_TILE // RET_TILE):
        rows = slice(s * RET_TILE, (s + 1) * RET_TILE)
        p0 = r.pos[0, pl.ds(tile_in_seq * (SEQ_TILE // LANES) + s * (RET_TILE // LANES), 1), 0:1]
        base = p0.astype(F32) * inv_freq
        cos_b = jnp.cos(base)
        sin_b = jnp.sin(base)
        for r0 in range(0, RET_TILE, ROW_BLOCK):
            blk = slice(r0, r0 + ROW_BLOCK)
            cosr = r.cosr[blk, :]
            sinr = r.sinr[blk, :]
            r.cos2[blk, :] = cosr * cos_b - sinr * sin_b
            r.sin2[blk, :] = sinr * (cos_b * sign) + cosr * (sin_b * sign)

        for hd in range(RET_HEADS):
            c0 = hd * HEAD_DIM
            q = r.proj[rows, _Q + c0:_Q + c0 + HEAD_DIM]
            k = r.proj[rows, _K + c0:_K + c0 + HEAD_DIM]
            v = r.proj[rows, _V + c0:_V + c0 + HEAD_DIM].astype(BF16)
            g = r.proj[rows, _G + c0:_G + c0 + HEAD_DIM]
            qr = q * r.cos2[...] + pltpu.roll(q, half, axis=1) * r.sin2[...]
            kr = k * r.cos2[...] + pltpu.roll(k, half, axis=1) * r.sin2[...]
            scores = lax.dot_general(qr.astype(BF16), kr.astype(BF16), (((1,), (1,)), ((), ())),
                                     preferred_element_type=F32)
            p = (scores * r.dmat[hd]).astype(BF16)
            state = r.state[hd]
            if s == 0:
                state = jnp.where(fresh, 0.0, state)
            o = _dot(p, v) + _dot((qr * r.qdec[hd]).astype(BF16), state.astype(BF16))
            kd_t = (kr * r.kdec[hd]).T.astype(BF16)
            r.state[hd] = state * math.exp(_LOG_GAMMA[hd] * RET_TILE) + _dot(kd_t, v)
            y = (g * jax.nn.sigmoid(g)) * (o * _rms_scale(o))
            r.ymix[rows, CONV_DIM + c0:CONV_DIM + c0 + HEAD_DIM] = y.astype(BF16)


def _mixer_kernel(x_hbm, mod, pos, gpre, gpost, win_hbm, convw, wout_hbm, out_hbm,
                  xbuf, obuf, sem_x, sem_o, win, wout, stage_in, stage_out, sem_in, sem_out,
                  h, proj, ymix, mix, state, ucarry, dmat, qdec, kdec, cosr, sinr, cos2, sin2):
    batch, seq, _ = x_hbm.shape
    tiles_per_seq = seq // SEQ_TILE
    n_tiles = batch * tiles_per_seq
    r = types.SimpleNamespace(
        mod=mod, gpre=gpre, gpost=gpost, convw=convw, h=h, proj=proj, ymix=ymix,
        mix=mix, state=state, ucarry=ucarry, dmat=dmat, qdec=qdec, kdec=kdec, cosr=cosr, sinr=sinr,
        cos2=cos2, sin2=sin2)

    def tile_of(hbm, i):
        return hbm.at[pl.ds(i // tiles_per_seq, 1), pl.ds((i % tiles_per_seq) * SEQ_TILE, SEQ_TILE), :]

    def x_copy(i, slot):
        return pltpu.make_async_copy(tile_of(x_hbm, i), xbuf.at[pl.ds(slot, 1)], sem_x.at[slot])

    def out_copy(i, slot):
        return pltpu.make_async_copy(obuf.at[pl.ds(slot, 1)], tile_of(out_hbm, i), sem_o.at[slot])

    x_copy(0, 0).start()
    _load_weight_bf16(win_hbm, win, stage_in, sem_in)
    _load_weight_bf16(wout_hbm, wout, stage_out, sem_out)
    _fill_tables(r)
    state[...] = jnp.zeros_like(state)
    ucarry[...] = jnp.zeros_like(ucarry)

    def step(i, carry):
        slot = i % 2

        @pl.when(i + 1 < n_tiles)
        def _():
            x_copy(i + 1, 1 - slot).start()

        x_copy(i, slot).wait()

        @pl.when(i >= 2)
        def _():
            out_copy(i - 2, slot).wait()

        r.batch = i // tiles_per_seq
        tile_in_seq = i % tiles_per_seq
        r.x = xbuf.at[pl.ds(slot, 1)]
        r.out = obuf.at[pl.ds(slot, 1)]
        r.pos = pos.at[pl.ds(r.batch, 1)]
        fresh = tile_in_seq == 0
        _norm(r)
        proj[...] = _dot(h[...], win[...])
        _conv(r, fresh)
        _retention(r, tile_in_seq, fresh)
        mix[...] = _dot(ymix[...], wout[...])
        _finish(r)
        out_copy(i, slot).start()
        return carry

    lax.fori_loop(0, n_tiles, step, 0)
    for i in range(max(n_tiles - 2, 0), n_tiles):
        out_copy(i, i % 2).wait()


def _mixer(x, mod, pos, g_pre, g_post, w_in, conv_w, w_out):
    tile = (SEQ_TILE, D_MODEL)
    vmem = pl.BlockSpec(memory_space=pltpu.VMEM)
    return pl.pallas_call(
        _mixer_kernel,
        in_specs=[
            pl.BlockSpec(memory_space=pl.ANY),
            vmem, vmem, vmem, vmem,
            pl.BlockSpec(memory_space=pl.ANY),
            vmem,
            pl.BlockSpec(memory_space=pl.ANY),
        ],
        out_specs=pl.BlockSpec(memory_space=pl.ANY),
        out_shape=jax.ShapeDtypeStruct(x.shape, F32),
        scratch_shapes=[
            pltpu.VMEM((2,) + tile, F32),
            pltpu.VMEM((2,) + tile, F32),
            pltpu.SemaphoreType.DMA((2,)),
            pltpu.SemaphoreType.DMA((2,)),
            pltpu.VMEM((D_MODEL, IN_DIM), BF16),
            pltpu.VMEM((D_MODEL, D_MODEL), BF16),
            pltpu.VMEM(_stage_shape(IN_DIM), F32),
            pltpu.VMEM(_stage_shape(D_MODEL), F32),
            pltpu.SemaphoreType.DMA((W_SLOTS,)),
            pltpu.SemaphoreType.DMA((W_SLOTS,)),
            pltpu.VMEM(tile, BF16),
            pltpu.VMEM((SEQ_TILE, IN_DIM), F32),
            pltpu.VMEM(tile, BF16),
            pltpu.VMEM(tile, F32),
            pltpu.VMEM((RET_HEADS, HEAD_DIM, HEAD_DIM), F32),
            pltpu.VMEM((SUBLANES, CONV_DIM), F32),
            pltpu.VMEM((RET_HEADS, RET_TILE, RET_TILE), F32),
            pltpu.VMEM((RET_HEADS, RET_TILE, HEAD_DIM), F32),
            pltpu.VMEM((RET_HEADS, RET_TILE, HEAD_DIM), F32),
            pltpu.VMEM((RET_TILE, HEAD_DIM), F32),
            pltpu.VMEM((RET_TILE, HEAD_DIM), F32),
            pltpu.VMEM((RET_TILE, HEAD_DIM), F32),
            pltpu.VMEM((RET_TILE, HEAD_DIM), F32),
        ],
        compiler_params=pltpu.CompilerParams(vmem_limit_bytes=VMEM_LIMIT),
        name="mixer",
    )(x, mod, pos, g_pre, g_post, w_in, conv_w, w_out)


def _mlp_kernel(x_ref, mod_ref, gpre_ref, gpost_ref, w1_hbm, w2_hbm, o_ref,
                w1_ref, w2_ref, stage1, stage2, sem1, sem2, act_ref):
    @pl.when(jnp.logical_and(pl.program_id(0) == 0, pl.program_id(1) == 0))
    def _():
        _load_weight_bf16(w1_hbm, w1_ref, stage1, sem1)
        _load_weight_bf16(w2_hbm, w2_ref, stage2, sem2)

    shift = _mod_row(mod_ref, 3)
    scale = _mod_row(mod_ref, 4)
    gate = _mod_row(mod_ref, 5)
    x = x_ref[0]
    h = ((x * _rms_scale(x)) * (gpre_ref[...] * (1.0 + scale)) + shift).astype(BF16)
    for j in range(D_FF // FF_BLOCK):
        a = jnp.maximum(_dot(h, w1_ref[:, j * FF_BLOCK:(j + 1) * FF_BLOCK]), 0.0)
        act_ref[:, j * FF_BLOCK:(j + 1) * FF_BLOCK] = (a * a).astype(BF16)
    f = _dot(act_ref[...], w2_ref[...])
    o_ref[0] = x_ref[0] + (f * _rms_scale(f)) * (gate * gpost_ref[...])


def _mlp(x, mod, g_pre, g_post, w1, w2):
    batch, seq, _ = x.shape
    return pl.pallas_call(
        _mlp_kernel,
        grid=(batch, seq // MLP_TILE),
        in_specs=[
            pl.BlockSpec((1, MLP_TILE, D_MODEL), lambda b, t: (b, t, 0)),
            pl.BlockSpec((N_MOD, batch, D_MODEL), lambda b, t: (0, 0, 0)),
            pl.BlockSpec((1, D_MODEL), lambda b, t: (0, 0)),
            pl.BlockSpec((1, D_MODEL), lambda b, t: (0, 0)),
            pl.BlockSpec(memory_space=pl.ANY),
            pl.BlockSpec(memory_space=pl.ANY),
        ],
        out_specs=pl.BlockSpec((1, MLP_TILE, D_MODEL), lambda b, t: (b, t, 0)),
        out_shape=jax.ShapeDtypeStruct(x.shape, F32),
        scratch_shapes=[
            pltpu.VMEM((D_MODEL, D_FF), BF16),
            pltpu.VMEM((D_FF, D_MODEL), BF16),
            pltpu.VMEM(_stage_shape(D_FF), F32),
            pltpu.VMEM(_stage_shape(D_MODEL), F32),
            pltpu.SemaphoreType.DMA((W_SLOTS,)),
            pltpu.SemaphoreType.DMA((W_SLOTS,)),
            pltpu.VMEM((MLP_TILE, D_FF), BF16),
        ],
        compiler_params=pltpu.CompilerParams(
            dimension_semantics=("arbitrary", "arbitrary"), vmem_limit_bytes=VMEM_LIMIT),
        name="mlp",
    )(x, mod, g_pre, g_post, w1, w2)


def kernel(x, c, positions, w_ada, b_ada, g_pre_mix, g_post_mix, w_in, conv_w, w_out,
           g_pre_mlp, g_post_mlp, w_fc1, w_fc2):
    batch, seq, d = x.shape
    assert d == D_MODEL and seq % SEQ_TILE == 0 and seq % MLP_TILE == 0
    assert SEQ_TILE % RET_TILE == 0 and RET_TILE % CHUNK == 0 and RET_TILE % LANES == 0
    pos = positions.reshape(batch, seq // LANES, LANES)
    for layer in range(w_ada.shape[0]):
        one = slice(layer, layer + 1)
        mod = _ada(c, w_ada[layer], b_ada[one])
        x = _mixer(x, mod, pos, g_pre_mix[one], g_post_mix[one], w_in[layer], conv_w[layer], w_out[layer])
        x = _mlp(x, mod, g_pre_mlp[one], g_post_mlp[one], w_fc1[layer], w_fc2[layer])
    return x
```

```python
import math
import types

import jax
import jax.numpy as jnp
from jax import lax
from jax.experimental import pallas as pl
from jax.experimental.pallas import tpu as pltpu

F32 = jnp.float32
BF16 = jnp.bfloat16

D_MODEL = 1024
CHUNK = 64
CONV_WIDTH = 3
CONV_DIM = D_MODEL // 2
RET_HEADS = 4
RET_DIM = D_MODEL - CONV_DIM
HEAD_DIM = RET_DIM // RET_HEADS
IN_DIM = 3 * CONV_DIM + 4 * RET_DIM
D_FF = 4 * D_MODEL
ROPE_BASE = 10000.0
DECAY_OFFSET = 5.0
EPS = 1e-6
N_MOD = 6

LANES = 128
SUBLANES = 8
SEQ_TILE = 1024
RET_TILE = 256
ROW_BLOCK = 32
MLP_TILE = 1024
FF_BLOCK = 512
ADA_BLOCK = 2048
W_SLOTS = 4
W_CHUNK_BYTES = 1 << 20
VMEM_LIMIT = 58 * 1024 * 1024

_XIN, _BG, _CG = 0, CONV_DIM, 2 * CONV_DIM
_Q = 3 * CONV_DIM
_K = _Q + RET_DIM
_V = _K + RET_DIM
_G = _V + RET_DIM

_LOG_GAMMA = [math.log1p(-2.0 ** (-DECAY_OFFSET - h)) for h in range(RET_HEADS)]


def _stage_shape(cols):
    rows = 1 << int(math.log2(W_CHUNK_BYTES // (4 * cols)))
    return (W_SLOTS, rows, cols)


def _dot(a, b):
    return jnp.dot(a, b, preferred_element_type=F32)


def _rms_scale(v):
    return lax.rsqrt(jnp.mean(v * v, axis=-1, keepdims=True) + EPS)


def _load_weight_bf16(w_hbm, w_vmem, stage, sem):
    slots, chunk = stage.shape[0], stage.shape[1]
    n_chunks = w_hbm.shape[0] // chunk

    def copy(k):
        return pltpu.make_async_copy(
            w_hbm.at[pl.ds(k * chunk, chunk), :], stage.at[k % slots], sem.at[k % slots])

    for k in range(min(slots - 1, n_chunks)):
        copy(k).start()
    for k in range(n_chunks):
        if k + slots - 1 < n_chunks:
            copy(k + slots - 1).start()
        copy(k).wait()
        w_vmem[k * chunk:(k + 1) * chunk, :] = stage[k % slots].astype(BF16)


def _walk_tiles(x_hbm, out_hbm, xbuf, obuf, sem_x, sem_o, setup, body):
    rows = xbuf.shape[1]
    tiles_per_seq = x_hbm.shape[1] // rows
    n_tiles = x_hbm.shape[0] * tiles_per_seq

    def tile_of(hbm, i):
        return hbm.at[pl.ds(i // tiles_per_seq, 1), pl.ds((i % tiles_per_seq) * rows, rows), :]

    def x_copy(i, slot):
        return pltpu.make_async_copy(tile_of(x_hbm, i), xbuf.at[pl.ds(slot, 1)], sem_x.at[slot])

    def out_copy(i, slot):
        return pltpu.make_async_copy(obuf.at[pl.ds(slot, 1)], tile_of(out_hbm, i), sem_o.at[slot])

    x_copy(0, 0).start()
    setup()

    def step(i, carry):
        slot = i % 2

        @pl.when(i + 1 < n_tiles)
        def _():
            x_copy(i + 1, 1 - slot).start()

        x_copy(i, slot).wait()

        @pl.when(i >= 2)
        def _():
            out_copy(i - 2, slot).wait()

        body(i // tiles_per_seq, i % tiles_per_seq, xbuf.at[pl.ds(slot, 1)], obuf.at[pl.ds(slot, 1)])
        out_copy(i, slot).start()
        return carry

    lax.fori_loop(0, n_tiles, step, 0)
    for i in range(max(n_tiles - 2, 0), n_tiles):
        out_copy(i, i % 2).wait()


def _ada_kernel(c_ref, w_ref, b_ref, o_ref):
    c = c_ref[...]
    a = (c * jax.nn.sigmoid(c)).astype(BF16)
    mod = _dot(a, w_ref[...].astype(BF16)) + b_ref[...]
    for m in range(ADA_BLOCK // D_MODEL):
        o_ref[m] = mod[:, m * D_MODEL:(m + 1) * D_MODEL]


def _ada(c, w_ada, b_ada):
    batch = c.shape[0]
    n = w_ada.shape[1]
    per_step = ADA_BLOCK // D_MODEL
    return pl.pallas_call(
        _ada_kernel,
        grid=(n // ADA_BLOCK,),
        in_specs=[
            pl.BlockSpec((batch, D_MODEL), lambda j: (0, 0)),
            pl.BlockSpec((D_MODEL, ADA_BLOCK), lambda j: (0, j)),
            pl.BlockSpec((1, ADA_BLOCK), lambda j: (0, j)),
        ],
        out_specs=pl.BlockSpec((per_step, batch, D_MODEL), lambda j: (j, 0, 0)),
        out_shape=jax.ShapeDtypeStruct((n // D_MODEL, batch, D_MODEL), F32),
        compiler_params=pltpu.CompilerParams(dimension_semantics=("arbitrary",)),
        name="ada_mod",
    )(c, w_ada, b_ada)


def _mod_row(mod_ref, k, b):
    return mod_ref[k, pl.ds(b, 1), :]


def _inv_freq():
    half = HEAD_DIM // 2
    lane = lax.broadcasted_iota(jnp.int32, (1, HEAD_DIM), 1)
    return jnp.exp((lane % half).astype(F32) * (-math.log(ROPE_BASE) / half))


def _fill_tables(r):
    k_scale = HEAD_DIM ** -0.5
    i = lax.broadcasted_iota(jnp.int32, (RET_TILE, RET_TILE), 0)
    j = lax.broadcasted_iota(jnp.int32, (RET_TILE, RET_TILE), 1)
    ci = i // CHUNK
    cj = j // CHUNK
    dist = jnp.where(ci == cj, jnp.abs(i - j), i - j).astype(F32)
    t = lax.broadcasted_iota(jnp.int32, (RET_TILE, HEAD_DIM), 0).astype(F32)
    for h in range(RET_HEADS):
        lg = _LOG_GAMMA[h]
        r.dmat[h] = jnp.where(cj <= ci, k_scale * jnp.exp(lg * dist), 0.0)
        r.qdec[h] = jnp.exp(lg * (t + 1.0))
        r.kdec[h] = k_scale * jnp.exp(lg * (RET_TILE - 1.0 - t))
    ang = t * _inv_freq()
    r.cosr[...] = jnp.cos(ang)
    r.sinr[...] = jnp.sin(ang)


def _norm(r):
    gain = r.gpre[...] * (1.0 + _mod_row(r.mod, 1, r.batch))
    shift = _mod_row(r.mod, 0, r.batch)
    for r0 in range(0, SEQ_TILE, ROW_BLOCK):
        rows = slice(r0, r0 + ROW_BLOCK)
        x = r.x[0, rows, :]
        r.h[rows, :] = ((x * _rms_scale(x)) * gain + shift).astype(BF16)


def _finish(r):
    gain = _mod_row(r.mod, 2, r.batch) * r.gpost[...]
    for r0 in range(0, SEQ_TILE, ROW_BLOCK):
        rows = slice(r0, r0 + ROW_BLOCK)
        mix = r.mix[rows, :]
        r.out[0, rows, :] = r.x[0, rows, :] + (mix * _rms_scale(mix)) * gain


def _conv(r, fresh):
    tail = jnp.where(fresh, 0.0, r.ucarry[...])
    w0, w1, w2 = r.convw[0:1, :], r.convw[1:2, :], r.convw[2:3, :]
    for r0 in range(0, SEQ_TILE, ROW_BLOCK):
        rows = slice(r0, r0 + ROW_BLOCK)
        u = r.proj[rows, _CG:_CG + CONV_DIM] * r.proj[rows, _XIN:_XIN + CONV_DIM]
        ext = jnp.concatenate([tail, u], axis=0)
        u1 = pltpu.roll(ext, 1, axis=0)[SUBLANES:]
        u2 = pltpu.roll(ext, 2, axis=0)[SUBLANES:]
        y_conv = r.proj[rows, _BG:_BG + CONV_DIM] * (w0 * u2 + w1 * u1 + w2 * u)
        r.ymix[rows, 0:CONV_DIM] = y_conv.astype(BF16)
        tail = u[ROW_BLOCK - SUBLANES:]
    r.ucarry[...] = tail


def _retention(r, tile_in_seq, fresh):
    half = HEAD_DIM // 2
    lane = lax.broadcasted_iota(jnp.int32, (1, HEAD_DIM), 1)
    sign = jnp.where(lane < half, -1.0, 1.0)
    inv_freq = _inv_freq()

    for s in range(SEQ_TILE // RET_TILE):
        rows = slice(s * RET_TILE, (s + 1) * RET_TILE)
        p0 = r.pos[0, pl.ds(tile_in_seq * (SEQ_TILE // LANES) + s * (RET_TILE // LANES), 1), 0:1]
        base = p0.astype(F32) * inv_freq
        cos_b = jnp.cos(base)
        sin_b = jnp.sin(base)
        for r0 in range(0, RET_TILE, ROW_BLOCK):
            blk = slice(r0, r0 + ROW_BLOCK)
            cosr = r.cosr[blk, :]
            sinr = r.sinr[blk, :]
            r.cos2[blk, :] = cosr * cos_b - sinr * sin_b
            r.sin2[blk, :] = sinr * (cos_b * sign) + cosr * (sin_b * sign)

        for hd in range(RET_HEADS):
            c0 = hd * HEAD_DIM
            q = r.proj[rows, _Q + c0:_Q + c0 + HEAD_DIM]
            k = r.proj[rows, _K + c0:_K + c0 + HEAD_DIM]
            v = r.proj[rows, _V + c0:_V + c0 + HEAD_DIM].astype(BF16)
            g = r.proj[rows, _G + c0:_G + c0 + HEAD_DIM]
            qr = q * r.cos2[...] + pltpu.roll(q, half, axis=1) * r.sin2[...]
            kr = k * r.cos2[...] + pltpu.roll(k, half, axis=1) * r.sin2[...]
            scores = lax.dot_general(qr.astype(BF16), kr.astype(BF16), (((1,), (1,)), ((), ())),
                                     preferred_element_type=F32)
            p = (scores * r.dmat[hd]).astype(BF16)
            state = r.state[hd]
            if s == 0:
                state = jnp.where(fresh, 0.0, state)
            o = _dot(p, v) + _dot((qr * r.qdec[hd]).astype(BF16), state.astype(BF16))
            kd_t = (kr * r.kdec[hd]).T.astype(BF16)
            r.state[hd] = state * math.exp(_LOG_GAMMA[hd] * RET_TILE) + _dot(kd_t, v)
            y = (g * jax.nn.sigmoid(g)) * (o * _rms_scale(o))
            r.ymix[rows, CONV_DIM + c0:CONV_DIM + c0 + HEAD_DIM] = y.astype(BF16)


def _mixer_kernel(x_hbm, mod, pos, gpre, gpost, win_hbm, convw, wout_hbm, out_hbm,
                  xbuf, obuf, sem_x, sem_o, win, wout, stage_in, stage_out, sem_in, sem_out,
                  h, proj, ymix, mix, state, ucarry, dmat, qdec, kdec, cosr, sinr, cos2, sin2):
    r = types.SimpleNamespace(
        mod=mod, gpre=gpre, gpost=gpost, convw=convw, h=h, proj=proj, ymix=ymix,
        mix=mix, state=state, ucarry=ucarry, dmat=dmat, qdec=qdec, kdec=kdec, cosr=cosr, sinr=sinr,
        cos2=cos2, sin2=sin2)

    def setup():
        _load_weight_bf16(win_hbm, win, stage_in, sem_in)
        _load_weight_bf16(wout_hbm, wout, stage_out, sem_out)
        _fill_tables(r)
        state[...] = jnp.zeros_like(state)
        ucarry[...] = jnp.zeros_like(ucarry)

    def body(batch_index, tile_in_seq, x_tile, out_tile):
        r.batch, r.x, r.out = batch_index, x_tile, out_tile
        r.pos = pos.at[pl.ds(batch_index, 1)]
        fresh = tile_in_seq == 0
        _norm(r)
        proj[...] = _dot(h[...], win[...])
        _conv(r, fresh)
        _retention(r, tile_in_seq, fresh)
        mix[...] = _dot(ymix[...], wout[...])
        _finish(r)

    _walk_tiles(x_hbm, out_hbm, xbuf, obuf, sem_x, sem_o, setup, body)


def _mixer(x, mod, pos, g_pre, g_post, w_in, conv_w, w_out):
    tile = (SEQ_TILE, D_MODEL)
    vmem = pl.BlockSpec(memory_space=pltpu.VMEM)
    return pl.pallas_call(
        _mixer_kernel,
        in_specs=[
            pl.BlockSpec(memory_space=pl.ANY),
            vmem, vmem, vmem, vmem,
            pl.BlockSpec(memory_space=pl.ANY),
            vmem,
            pl.BlockSpec(memory_space=pl.ANY),
        ],
        out_specs=pl.BlockSpec(memory_space=pl.ANY),
        out_shape=jax.ShapeDtypeStruct(x.shape, F32),
        scratch_shapes=[
            pltpu.VMEM((2,) + tile, F32),
            pltpu.VMEM((2,) + tile, F32),
            pltpu.SemaphoreType.DMA((2,)),
            pltpu.SemaphoreType.DMA((2,)),
            pltpu.VMEM((D_MODEL, IN_DIM), BF16),
            pltpu.VMEM((D_MODEL, D_MODEL), BF16),
            pltpu.VMEM(_stage_shape(IN_DIM), F32),
            pltpu.VMEM(_stage_shape(D_MODEL), F32),
            pltpu.SemaphoreType.DMA((W_SLOTS,)),
            pltpu.SemaphoreType.DMA((W_SLOTS,)),
            pltpu.VMEM(tile, BF16),
            pltpu.VMEM((SEQ_TILE, IN_DIM), F32),
            pltpu.VMEM(tile, BF16),
            pltpu.VMEM(tile, F32),
            pltpu.VMEM((RET_HEADS, HEAD_DIM, HEAD_DIM), F32),
            pltpu.VMEM((SUBLANES, CONV_DIM), F32),
            pltpu.VMEM((RET_HEADS, RET_TILE, RET_TILE), F32),
            pltpu.VMEM((RET_HEADS, RET_TILE, HEAD_DIM), F32),
            pltpu.VMEM((RET_HEADS, RET_TILE, HEAD_DIM), F32),
            pltpu.VMEM((RET_TILE, HEAD_DIM), F32),
            pltpu.VMEM((RET_TILE, HEAD_DIM), F32),
            pltpu.VMEM((RET_TILE, HEAD_DIM), F32),
            pltpu.VMEM((RET_TILE, HEAD_DIM), F32),
        ],
        compiler_params=pltpu.CompilerParams(vmem_limit_bytes=VMEM_LIMIT),
        name="mixer",
    )(x, mod, pos, g_pre, g_post, w_in, conv_w, w_out)


def _mlp_kernel(x_hbm, mod_ref, gpre_ref, gpost_ref, w1_hbm, w2_hbm, out_hbm,
                xbuf, obuf, sem_x, sem_o, w1_ref, w2_ref, stage1, stage2, sem1, sem2, act_ref):
    def setup():
        _load_weight_bf16(w1_hbm, w1_ref, stage1, sem1)
        _load_weight_bf16(w2_hbm, w2_ref, stage2, sem2)

    def body(batch_index, tile_in_seq, x_ref, o_ref):
        shift = _mod_row(mod_ref, 3, batch_index)
        scale = _mod_row(mod_ref, 4, batch_index)
        gate = _mod_row(mod_ref, 5, batch_index)
        x = x_ref[0]
        h = ((x * _rms_scale(x)) * (gpre_ref[...] * (1.0 + scale)) + shift).astype(BF16)
        for j in range(D_FF // FF_BLOCK):
            a = jnp.maximum(_dot(h, w1_ref[:, j * FF_BLOCK:(j + 1) * FF_BLOCK]), 0.0)
            act_ref[:, j * FF_BLOCK:(j + 1) * FF_BLOCK] = (a * a).astype(BF16)
        f = _dot(act_ref[...], w2_ref[...])
        o_ref[0] = x_ref[0] + (f * _rms_scale(f)) * (gate * gpost_ref[...])

    _walk_tiles(x_hbm, out_hbm, xbuf, obuf, sem_x, sem_o, setup, body)


def _mlp(x, mod, g_pre, g_post, w1, w2):
    tile = (MLP_TILE, D_MODEL)
    vmem = pl.BlockSpec(memory_space=pltpu.VMEM)
    return pl.pallas_call(
        _mlp_kernel,
        in_specs=[
            pl.BlockSpec(memory_space=pl.ANY),
            vmem, vmem, vmem,
            pl.BlockSpec(memory_space=pl.ANY),
            pl.BlockSpec(memory_space=pl.ANY),
        ],
        out_specs=pl.BlockSpec(memory_space=pl.ANY),
        out_shape=jax.ShapeDtypeStruct(x.shape, F32),
        scratch_shapes=[
            pltpu.VMEM((2,) + tile, F32),
            pltpu.VMEM((2,) + tile, F32),
            pltpu.SemaphoreType.DMA((2,)),
            pltpu.SemaphoreType.DMA((2,)),
            pltpu.VMEM((D_MODEL, D_FF), BF16),
            pltpu.VMEM((D_FF, D_MODEL), BF16),
            pltpu.VMEM(_stage_shape(D_FF), F32),
            pltpu.VMEM(_stage_shape(D_MODEL), F32),
            pltpu.SemaphoreType.DMA((W_SLOTS,)),
            pltpu.SemaphoreType.DMA((W_SLOTS,)),
            pltpu.VMEM((MLP_TILE, D_FF), BF16),
        ],
        compiler_params=pltpu.CompilerParams(vmem_limit_bytes=VMEM_LIMIT),
        name="mlp",
    )(x, mod, g_pre, g_post, w1, w2)


def kernel(x, c, positions, w_ada, b_ada, g_pre_mix, g_post_mix, w_in, conv_w, w_out,
           g_pre_mlp, g_post_mlp, w_fc1, w_fc2):
    batch, seq, d = x.shape
    assert d == D_MODEL and seq % SEQ_TILE == 0 and seq % MLP_TILE == 0
    assert SEQ_TILE % RET_TILE == 0 and RET_TILE % CHUNK == 0 and RET_TILE % LANES == 0
    pos = positions.reshape(batch, seq // LANES, LANES)
    for layer in range(w_ada.shape[0]):
        one = slice(layer, layer + 1)
        mod = _ada(c, w_ada[layer], b_ada[one])
        x = _mixer(x, mod, pos, g_pre_mix[one], g_post_mix[one], w_in[layer], conv_w[layer], w_out[layer])
        x = _mlp(x, mod, g_pre_mlp[one], g_post_mlp[one], w_fc1[layer], w_fc2[layer])
    return x
```
